```python
import jax, jax.numpy as jnp
from jax import lax
import numpy as np

D_MODEL = 2048
BATCH = 32
SEQ = 256
DEPTH = 4
DEC_BATCH = 8
DEC_SEQ = 1024
PAST_LEN = 512

GRID_W = 64
HEAD_DIM = 128
N_GROUPS = 4
GROUP_W = D_MODEL // N_GROUPS
GROUP_HEADS = GROUP_W // HEAD_DIM
NORM_EPS = 1e-6
NEG_INF = -1e30

LRU_W = GROUP_W
LRU_BLOCKS = GROUP_HEADS
LRU_BLOCK = LRU_W // LRU_BLOCKS
LRU_C = 8.0
CONV_W = 4
CONV_PAD = (1, 2)

NA_HEADS = GROUP_HEADS
NA_ROWS = 8
NA_COLS = 16
NA_QC = 16
NA_KC = 32

GQA_HEADS = GROUP_HEADS
GQA_KV_HEADS = 2
KV_W = GQA_KV_HEADS * HEAD_DIM
ROPE_THETA = 10000.0
Q_BLOCK = 128

ML_HEADS = GROUP_HEADS
ML_CHUNK = 64

IN_LAYOUT = (
    ('lru_x', LRU_W), ('lru_gate', LRU_W),
    ('na_q', GROUP_W), ('na_k', GROUP_W), ('na_v', GROUP_W), ('na_gate', GROUP_W),
    ('gqa_q', GROUP_W), ('gqa_k', KV_W), ('gqa_v', KV_W), ('gqa_gate', GROUP_W),
    ('ml_q', GROUP_W), ('ml_k', GROUP_W), ('ml_v', GROUP_W), ('ml_o', GROUP_W), ('ml_gate', GROUP_W),
    ('ml_if', 4 * ML_HEADS),
)
IN_DIM = sum(width for _, width in IN_LAYOUT)

kernel_name = 'hybrid_diffusion_parallel_heads_step'


def rmsnorm(x, w):
    xf = x.astype(jnp.float32)
    y = xf * lax.rsqrt(jnp.mean(xf * xf, axis=-1, keepdims=True) + NORM_EPS)
    return (y * w.astype(jnp.float32)).astype(x.dtype)


def split_projection(p):
    out, start = {}, 0
    for name, width in IN_LAYOUT:
        out[name] = p[..., start:start + width]
        start += width
    return out


def dwconv(x, w, b):
    y = lax.conv_general_dilated(x, w[:, None, :], window_strides=(1,), padding=[CONV_PAD],
                                 dimension_numbers=('NWC', 'WIO', 'NWC'), feature_group_count=x.shape[-1])
    return y + b


def linear_scan(a, b, h0, reverse):
    if reverse:
        a, b = jnp.flip(a, 1), jnp.flip(b, 1)
    b = b.at[:, 0].add(a[:, 0] * h0)

    def comb(l, r):
        return (l[0] * r[0], r[0] * l[1] + r[1])

    _, h = lax.associative_scan(comb, (a, b), axis=1)
    h_last = h[:, -1]
    if reverse:
        h = jnp.flip(h, 1)
    return h, h_last


def rglru(xc, w_r, b_r, w_i, b_i, lam, h0, reverse):
    B, T, W = xc.shape
    xb = xc.reshape(B, T, LRU_BLOCKS, LRU_BLOCK)
    r = jax.nn.sigmoid((jnp.einsum('btgi,gij->btgj', xb, w_r).reshape(B, T, W) + b_r).astype(jnp.float32))
    i = jax.nn.sigmoid((jnp.einsum('btgi,gij->btgj', xb, w_i).reshape(B, T, W) + b_i).astype(jnp.float32))
    log_a = LRU_C * r * jax.nn.log_sigmoid(lam.astype(jnp.float32))
    a = jnp.exp(log_a)
    b = jnp.sqrt(-jnp.expm1(2.0 * log_a)) * i * xc.astype(jnp.float32)
    return linear_scan(a, b, h0.astype(jnp.float32), reverse)


def blocked_attention(q, k, v):
    B, Tq, H, d = q.shape
    Hkv = k.shape[2]
    g = H // Hkv
    nb = Tq // Q_BLOCK
    qb = jnp.moveaxis(q.reshape(B, nb, Q_BLOCK, Hkv, g, d), 1, 0)

    def one_block(qblk):
        s = jnp.einsum('bqkgd,bskd->bkgqs', qblk, k, preferred_element_type=jnp.float32) * (d ** -0.5)
        p = jax.nn.softmax(s, axis=-1).astype(v.dtype)
        return jnp.einsum('bkgqs,bskd->bqkgd', p, v)

    o = lax.map(one_block, qb)
    return jnp.moveaxis(o, 0, 1).reshape(B, Tq, H, d)


def axial_rope(x):
    B, S, H, d = x.shape
    t = jnp.arange(S)
    half = d // 2
    nf = half // 2
    inv = ROPE_THETA ** (-jnp.arange(nf, dtype=jnp.float32) / nf)
    xf = x.astype(jnp.float32)

    def rotate(xa, pos):
        ang = pos.astype(jnp.float32)[:, None] * inv
        cos = jnp.cos(ang)[None, :, None, :]
        sin = jnp.sin(ang)[None, :, None, :]
        x1, x2 = xa[..., :nf], xa[..., nf:]
        return jnp.concatenate([x1 * cos - x2 * sin, x2 * cos + x1 * sin], axis=-1)

    out = jnp.concatenate([rotate(xf[..., :half], t // GRID_W), rotate(xf[..., half:], t % GRID_W)], axis=-1)
    return out.astype(x.dtype)


def natten_latent(q, k, v, kc, vc, rpb):
    B, S, H, d = q.shape
    rows = S // GRID_W
    wr = min(NA_ROWS, rows)
    ncb = GRID_W // NA_QC
    r = np.arange(rows)
    row_idx = np.clip(r - wr // 2, 0, rows - wr)[:, None] + np.arange(wr)
    cb = np.arange(ncb)
    col_idx = np.clip(cb * NA_QC - (NA_KC - NA_QC) // 2, 0, GRID_W - NA_KC)[:, None] + np.arange(NA_KC)
    qcol = cb[:, None] * NA_QC + np.arange(NA_QC)
    qcol_start = np.clip(qcol - NA_COLS // 2, 0, GRID_W - NA_COLS)
    valid = (col_idx[:, None, :] >= qcol_start[..., None]) & (col_idx[:, None, :] < qcol_start[..., None] + NA_COLS)
    row_off = row_idx - r[:, None] + NA_ROWS - 1
    col_off = np.clip(col_idx[:, None, :] - qcol[..., None] + NA_COLS - 1, 0, 2 * NA_COLS - 2)
    bias = rpb[:, row_off[:, None, None, :, None], col_off[None, :, :, None, :]].astype(jnp.float32)
    bias = jnp.where(valid[None, None, :, :, None, :], bias, NEG_INF)
    n_loc = wr * NA_KC
    bias = bias.reshape(H, rows, ncb, NA_QC, n_loc)
    kg = k.reshape(B, rows, GRID_W, H, d)
    vg = v.reshape(B, rows, GRID_W, H, d)
    gidx = (row_idx[:, None, :, None], col_idx[None, :, None, :])
    k_blk = kg[:, gidx[0], gidx[1]].reshape(B, rows, ncb, n_loc, H, d)
    v_blk = vg[:, gidx[0], gidx[1]].reshape(B, rows, ncb, n_loc, H, d)
    qb = q.reshape(B, rows, ncb, NA_QC, H, d)
    scale = d ** -0.5
    s_loc = jnp.einsum('brcqhd,brckhd->bhrcqk', qb, k_blk, preferred_element_type=jnp.float32) * scale + bias
    s_ctx = jnp.einsum('brcqhd,blhd->bhrcql', qb, kc, preferred_element_type=jnp.float32) * scale
    p = jax.nn.softmax(jnp.concatenate([s_loc, s_ctx], axis=-1), axis=-1).astype(v.dtype)
    o = (jnp.einsum('bhrcqk,brckhd->brcqhd', p[..., :n_loc], v_blk)
         + jnp.einsum('bhrcql,blhd->brcqhd', p[..., n_loc:], vc))
    return o.reshape(B, S, H, d)


def mlstm_chunkwise(q, k, v, li, lf, C0, n0, m0, reverse):
    B, T, H, d = q.shape
    if reverse:
        q, k, v, li, lf = (jnp.flip(a, 1) for a in (q, k, v, li, lf))
    nc = T // ML_CHUNK

    def chunks(a):
        a = a.astype(jnp.float32).reshape(B, nc, ML_CHUNK, H, *a.shape[3:])
        return jnp.moveaxis(a, (1, 3), (0, 2))

    qs = chunks(q) * (d ** -0.5)
    ks, vs, lis, lfs = chunks(k), chunks(v), chunks(li), chunks(lf)
    tri = jnp.tril(jnp.ones((ML_CHUNK, ML_CHUNK), dtype=bool))

    def step(carry, inp):
        C, n, m = carry
        qq, kk, vv, ii, ff = inp
        b = jnp.cumsum(ff, axis=-1)
        log_w = jnp.where(tri, b[..., :, None] - b[..., None, :] + ii[..., None, :], NEG_INF)
        m_inter = b + m[..., None]
        m_t = jnp.maximum(m_inter, jnp.max(log_w, axis=-1))
        w_inter = jnp.exp(m_inter - m_t)
        scores = jnp.einsum('bhtd,bhsd->bhts', qq, kk) * jnp.exp(log_w - m_t[..., None])
        num = jnp.einsum('bhts,bhse->bhte', scores, vv) + w_inter[..., None] * jnp.einsum('bhtd,bhde->bhte', qq, C)
        den = jnp.sum(scores, axis=-1) + w_inter * jnp.einsum('bhtd,bhd->bht', qq, n)
        h = num / jnp.maximum(jnp.abs(den), jnp.exp(-m_t))[..., None]
        b_last = b[..., -1]
        log_u = b_last[..., None] - b + ii
        m_new = jnp.maximum(b_last + m, jnp.max(log_u, axis=-1))
        w_prev = jnp.exp(b_last + m - m_new)
        u = jnp.exp(log_u - m_new[..., None])
        C_new = w_prev[..., None, None] * C + jnp.einsum('bhs,bhsd,bhse->bhde', u, kk, vv)
        n_new = w_prev[..., None] * n + jnp.einsum('bhs,bhsd->bhd', u, kk)
        return (C_new, n_new, m_new), h

    init = (C0.astype(jnp.float32), n0.astype(jnp.float32), m0.astype(jnp.float32))
    (C, n, m), h = lax.scan(step, init, (qs, ks, vs, lis, lfs))
    h = jnp.moveaxis(h, (0, 2), (1, 3)).reshape(B, T, H, d)
    if reverse:
        h = jnp.flip(h, 1)
    return h, (C, n, m)


def trunk_layer(x, cond, lp, ctx):
    B, T, _ = x.shape
    f32 = jnp.float32
    mod = jax.nn.silu(cond) @ lp['ada_w'] + lp['ada_b']
    if mod.ndim == 2:
        mod = mod[:, None, :]
    shift, scale, gate = jnp.split(mod, 3, axis=-1)
    xm = rmsnorm(x, lp['norm_w']) * (1.0 + scale) + shift
    pr = split_projection(xm @ lp['w_in'])

    if ctx is None:
        lru_h0 = jnp.zeros((B, 2, LRU_W), f32)
        C0 = jnp.zeros((B, 2, ML_HEADS, HEAD_DIM, HEAD_DIM), f32)
        n0 = jnp.zeros((B, 2, ML_HEADS, HEAD_DIM), f32)
        m0 = jnp.zeros((B, 2, ML_HEADS), f32)
    else:
        na_kc, na_vc, gqa_kc, gqa_vc, lru_h0, C0, n0, m0 = ctx

    xa = dwconv(pr['lru_x'], lp['lru_conv_w'], lp['lru_conv_b'])
    ha_f, sa_f = rglru(xa, lp['lru_wr'][0], lp['lru_br'][0], lp['lru_wi'][0], lp['lru_bi'][0],
                       lp['lru_lambda'][0], lru_h0[:, 0], False)
    ha_b, sa_b = rglru(xa, lp['lru_wr'][1], lp['lru_br'][1], lp['lru_wi'][1], lp['lru_bi'][1],
                       lp['lru_lambda'][1], lru_h0[:, 1], True)
    out_a = (ha_f + ha_b).astype(x.dtype) * jax.nn.silu(pr['lru_gate'])

    nq = pr['na_q'].reshape(B, T, NA_HEADS, HEAD_DIM)
    nk = pr['na_k'].reshape(B, T, NA_HEADS, HEAD_DIM)
    nv = pr['na_v'].reshape(B, T, NA_HEADS, HEAD_DIM)
    if ctx is None:
        ob = blocked_attention(nq, nk, nv)
    else:
        ob = natten_latent(nq, nk, nv, na_kc, na_vc, lp['na_rpb'])
    out_b = ob.reshape(B, T, GROUP_W) * jax.nn.silu(pr['na_gate'])

    gq = rmsnorm(pr['gqa_q'].reshape(B, T, GQA_HEADS, HEAD_DIM), lp['gqa_qnorm'])
    gk = rmsnorm(pr['gqa_k'].reshape(B, T, GQA_KV_HEADS, HEAD_DIM), lp['gqa_knorm'])
    gv = pr['gqa_v'].reshape(B, T, GQA_KV_HEADS, HEAD_DIM)
    if ctx is None:
        oc = blocked_attention(gq, gk, gv)
    else:
        oc = blocked_attention(axial_rope(gq), jnp.concatenate([axial_rope(gk), gqa_kc], axis=1),
                               jnp.concatenate([gv, gqa_vc], axis=1))
    out_c = oc.reshape(B, T, GROUP_W) * jax.nn.silu(pr['gqa_gate'])

    mq = pr['ml_q'].reshape(B, T, ML_HEADS, HEAD_DIM)
    mk = pr['ml_k'].reshape(B, T, ML_HEADS, HEAD_DIM)
    mv = pr['ml_v'].reshape(B, T, ML_HEADS, HEAD_DIM)
    gates = (pr['ml_if'] + lp['ml_gate_b']).astype(f32).reshape(B, T, 2, 2, ML_HEADS)
    h_sum = jnp.zeros((B, T, ML_HEADS, HEAD_DIM), f32)
    ml_states = []
    for dr in range(2):
        h_dir, st = mlstm_chunkwise(mq, mk, mv, gates[:, :, dr, 0], jax.nn.log_sigmoid(gates[:, :, dr, 1]),
                                    C0[:, dr], n0[:, dr], m0[:, dr], dr == 1)
        h_sum = h_sum + h_dir
        ml_states.append(st)
    hd = rmsnorm(h_sum, lp['ml_out_norm'].reshape(ML_HEADS, HEAD_DIM)).reshape(B, T, GROUP_W).astype(x.dtype)
    out_d = hd * jax.nn.sigmoid(pr['ml_o']) * jax.nn.silu(pr['ml_gate'])

    y = jnp.concatenate([out_a, out_b, out_c, out_d], axis=-1) @ lp['w_out']
    x = x + gate * y
    if ctx is not None:
        return x, None
    new_ctx = (nk, nv, gk, gv,
               jnp.stack([sa_f, sa_b], axis=1),
               jnp.stack([ml_states[0][0], ml_states[1][0]], axis=1),
               jnp.stack([ml_states[0][1], ml_states[1][1]], axis=1),
               jnp.stack([ml_states[0][2], ml_states[1][2]], axis=1))
    return x, new_ctx


def setup_inputs(seed: int = 0) -> dict:
    key = jax.random.key(seed)
    ks = jax.random.split(key, 40)
    f32 = jnp.float32

    def nrm(k, shape, s):
        return s * jax.random.normal(k, shape, f32)

    D = D_MODEL
    a0 = jax.random.uniform(ks[20], (DEPTH, 2, LRU_W), f32, minval=0.9, maxval=0.999)
    sig = a0 ** (1.0 / LRU_C)
    lru_lambda = jnp.log(sig) - jnp.log1p(-sig)
    i_bias = nrm(ks[21], (DEPTH, 2, 1, ML_HEADS), 0.1)
    f_bias = jnp.linspace(3.0, 6.0, ML_HEADS, dtype=f32) + nrm(ks[22], (DEPTH, 2, 1, ML_HEADS), 0.02)
    ml_gate_b = jnp.concatenate([i_bias, f_bias], axis=2).reshape(DEPTH, 4 * ML_HEADS)
    return {
        'x_prompt': nrm(ks[0], (BATCH, SEQ, D), 1.0),
        'x_sample': nrm(ks[1], (DEC_BATCH, DEC_SEQ, D), 1.0),
        'cache_na_k': nrm(ks[2], (DEC_BATCH, DEPTH, PAST_LEN, NA_HEADS, HEAD_DIM), 1.0),
        'cache_na_v': nrm(ks[3], (DEC_BATCH, DEPTH, PAST_LEN, NA_HEADS, HEAD_DIM), 1.0),
        'cache_gqa_k': nrm(ks[4], (DEC_BATCH, DEPTH, PAST_LEN, GQA_KV_HEADS, HEAD_DIM), 1.0),
        'cache_gqa_v': nrm(ks[5], (DEC_BATCH, DEPTH, PAST_LEN, GQA_KV_HEADS, HEAD_DIM), 1.0),
        'state_lru': nrm(ks[6], (DEC_BATCH, DEPTH, 2, LRU_W), 1.0),
        'state_mlstm_C': nrm(ks[7], (DEC_BATCH, DEPTH, 2, ML_HEADS, HEAD_DIM, HEAD_DIM), 0.3),
        'state_mlstm_n': nrm(ks[8], (DEC_BATCH, DEPTH, 2, ML_HEADS, HEAD_DIM), 0.3),
        'state_mlstm_m': 1.0 + nrm(ks[9], (DEC_BATCH, DEPTH, 2, ML_HEADS), 0.5),
        'c': nrm(ks[10], (DEC_BATCH, D), 1.0),
        'c_ctx': nrm(ks[11], (D,), 1.0),
        'norm_w': 1.0 + nrm(ks[12], (DEPTH, D), 0.02),
        'ada_w': nrm(ks[13], (DEPTH, D, 3 * D), D ** -0.5),
        'ada_b': nrm(ks[14], (DEPTH, 3 * D), 0.02),
        'w_in': nrm(ks[15], (DEPTH, D, IN_DIM), D ** -0.5),
        'lru_conv_w': nrm(ks[16], (DEPTH, CONV_W, LRU_W), CONV_W ** -0.5),
        'lru_conv_b': nrm(ks[17], (DEPTH, LRU_W), 0.02),
        'lru_wr': nrm(ks[18], (DEPTH, 2, LRU_BLOCKS, LRU_BLOCK, LRU_BLOCK), LRU_BLOCK ** -0.5),
        'lru_br': nrm(ks[19], (DEPTH, 2, LRU_W), 0.02),
        'lru_wi': nrm(ks[23], (DEPTH, 2, LRU_BLOCKS, LRU_BLOCK, LRU_BLOCK), LRU_BLOCK ** -0.5),
        'lru_bi': nrm(ks[24], (DEPTH, 2, LRU_W), 0.02),
        'lru_lambda': lru_lambda,
        'na_rpb': nrm(ks[25], (DEPTH, NA_HEADS, 2 * NA_ROWS - 1, 2 * NA_COLS - 1), 0.1),
        'gqa_qnorm': 1.0 + nrm(ks[26], (DEPTH, HEAD_DIM), 0.02),
        'gqa_knorm': 1.0 + nrm(ks[27], (DEPTH, HEAD_DIM), 0.02),
        'ml_gate_b': ml_gate_b,
        'ml_out_norm': 1.0 + nrm(ks[28], (DEPTH, GROUP_W), 0.02),
        'w_out': nrm(ks[29], (DEPTH, D, D), D ** -0.5),
        'final_norm_w': 1.0 + nrm(ks[30], (D,), 0.02),
    }


def reference(x_prompt, x_sample, cache_na_k, cache_na_v, cache_gqa_k, cache_gqa_v, state_lru,
              state_mlstm_C, state_mlstm_n, state_mlstm_m, c, c_ctx, norm_w, ada_w, ada_b, w_in,
              lru_conv_w, lru_conv_b, lru_wr, lru_br, lru_wi, lru_bi, lru_lambda, na_rpb,
              gqa_qnorm, gqa_knorm, ml_gate_b, ml_out_norm, w_out, final_norm_w):
    xp, xs = x_prompt, x_sample
    per_layer = []
    for l in range(DEPTH):
        lp = dict(norm_w=norm_w[l], ada_w=ada_w[l], ada_b=ada_b[l], w_in=w_in[l],
                  lru_conv_w=lru_conv_w[l], lru_conv_b=lru_conv_b[l], lru_wr=lru_wr[l], lru_br=lru_br[l],
                  lru_wi=lru_wi[l], lru_bi=lru_bi[l], lru_lambda=lru_lambda[l], na_rpb=na_rpb[l],
                  gqa_qnorm=gqa_qnorm[l], gqa_knorm=gqa_knorm[l], ml_gate_b=ml_gate_b[l],
                  ml_out_norm=ml_out_norm[l], w_out=w_out[l])
        xp, ctx_l = trunk_layer(xp, c_ctx, lp, None)
        per_layer.append(ctx_l)
        cached = (cache_na_k[:, l], cache_na_v[:, l], cache_gqa_k[:, l], cache_gqa_v[:, l], state_lru[:, l],
                  state_mlstm_C[:, l], state_mlstm_n[:, l], state_mlstm_m[:, l])
        xs, _ = trunk_layer(xs, c, lp, cached)
    y_prompt = rmsnorm(xp, final_norm_w)
    y_sample = rmsnorm(xs, final_norm_w)
    (new_na_k, new_na_v, new_gqa_k, new_gqa_v, new_lru,
     new_mlstm_C, new_mlstm_n, new_mlstm_m) = (jnp.stack([s[j] for s in per_layer], axis=1) for j in range(8))
    return (y_prompt, y_sample, new_na_k, new_na_v, new_gqa_k, new_gqa_v, new_lru,
            new_mlstm_C, new_mlstm_n, new_mlstm_m)
```

```python
import functools

import numpy as np
import jax
import jax.numpy as jnp
from jax import lax
from jax.experimental import pallas as pl
from jax.experimental.pallas import tpu as pltpu

F32 = jnp.float32
BF16 = jnp.bfloat16

HEAD_DIM = 128
GROUP_W = 512
HEADS = GROUP_W // HEAD_DIM
GQA_KV_HEADS = 2
GRID_W = 64
NORM_EPS = 1e-6
NEG_INF = -1e30
LRU_C = 8.0
NA_ROWS, NA_COLS = 8, 16
ROPE_THETA = 10000.0
ML_CHUNK = 256
COND_ROWS = 16
ROW_CHUNK = 256
VMEM_LIMIT = 52 * 1024 * 1024

COL_LRU_X, COL_LRU_G, COL_NA_Q, COL_NA_K, COL_NA_V, COL_NA_G = 0, 1, 2, 3, 4, 5
COL_GQA_Q, COL_GQA_KV, COL_GQA_G = 6, 7, 8
COL_ML_Q, COL_ML_K, COL_ML_V, COL_ML_O, COL_ML_G = 9, 10, 11, 12, 13
MAIN_W = 14 * GROUP_W


def _params(*sem):
    return pltpu.CompilerParams(dimension_semantics=sem, vmem_limit_bytes=VMEM_LIMIT)


def _sigmoid(x):
    return 1.0 / (1.0 + jnp.exp(-x))


def _silu(x):
    return x * _sigmoid(x)


def _log_sigmoid(x):
    return jnp.minimum(x, 0.0) - jnp.log1p(jnp.exp(-jnp.abs(x)))


def _dot(a, b):
    return jnp.dot(a, b, preferred_element_type=F32)


def _dot_nt(a, b):
    return lax.dot_general(a, b, (((1,), (1,)), ((), ())), preferred_element_type=F32)


def _dot_tn(a, b):
    return lax.dot_general(a, b, (((0,), (0,)), ((), ())), preferred_element_type=F32)


def _split3(x):
    x1 = x.astype(BF16)
    r = x - x1.astype(F32)
    x2 = r.astype(BF16)
    x3 = (r - x2.astype(F32)).astype(BF16)
    return x1, x2, x3


def _mod_kernel(cond_ref, w_ref, b_ref, o_ref):
    s = _silu(cond_ref[...]).astype(BF16)
    o_ref[...] = _dot(s, w_ref[...].astype(BF16)) + b_ref[...]


def _modulation(cond, ada_w, ada_b):
    depth, d, n = ada_w.shape
    tn = 1024
    return pl.pallas_call(
        _mod_kernel,
        out_shape=jax.ShapeDtypeStruct((depth, COND_ROWS, n), F32),
        grid=(depth, n // tn),
        in_specs=[pl.BlockSpec((COND_ROWS, d), lambda l, j: (0, 0)),
                  pl.BlockSpec((None, d, tn), lambda l, j: (l, 0, j)),
                  pl.BlockSpec((None, 1, tn), lambda l, j: (l, 0, j))],
        out_specs=pl.BlockSpec((None, COND_ROWS, tn), lambda l, j: (l, 0, j)),
        compiler_params=_params("parallel", "parallel"),
        name="adaln_mod",
    )(cond, ada_w, ada_b.reshape(depth, 1, n))


def _inproj_kernel(x_ref, nw_ref, sh_ref, sc_ref, w_ref, wif_ref, o_ref, oif_ref, xm_ref):
    @pl.when(pl.program_id(1) == 0)
    def _():
        for r in range(x_ref.shape[0] // ROW_CHUNK):
            rows = slice(r * ROW_CHUNK, (r + 1) * ROW_CHUNK)
            x = x_ref[rows, :]
            y = x * lax.rsqrt(jnp.mean(x * x, axis=-1, keepdims=True) + NORM_EPS) * nw_ref[...]
            xm_ref[rows, :] = (y * (1.0 + sc_ref[...]) + sh_ref[...]).astype(BF16)
        oif_ref[...] = _dot(xm_ref[...], wif_ref[...])

    o_ref[...] = _dot(xm_ref[...], w_ref[...])


def _in_projection(x, norm_w, mod4, layer, mod_row0, rows_per_mod, w_main, w_if):
    m, d = x.shape
    tm, tn = min(1024, m), 1024
    assert m % tm == 0 and rows_per_mod % tm == 0
    per_batch = rows_per_mod < m

    def mod_map(col):
        if per_batch:
            return lambda i, j: (layer, mod_row0 + (i * tm) // rows_per_mod, 0, col)
        return lambda i, j: (layer, mod_row0, 0, col)

    return pl.pallas_call(
        _inproj_kernel,
        out_shape=(jax.ShapeDtypeStruct((m, MAIN_W), F32), jax.ShapeDtypeStruct((m, 128), F32)),
        grid=(m // tm, MAIN_W // tn),
        in_specs=[pl.BlockSpec((tm, d), lambda i, j: (i, 0)),
                  pl.BlockSpec((1, d), lambda i, j: (0, 0)),
                  pl.BlockSpec((None, None, 1, d), mod_map(0)),
                  pl.BlockSpec((None, None, 1, d), mod_map(1)),
                  pl.BlockSpec((d, tn), lambda i, j: (0, j)),
                  pl.BlockSpec((d, 128), lambda i, j: (0, 0))],
        out_specs=(pl.BlockSpec((tm, tn), lambda i, j: (i, j)),
                   pl.BlockSpec((tm, 128), lambda i, j: (i, 0))),
        scratch_shapes=[pltpu.VMEM((tm, d), BF16)],
        compiler_params=_params("parallel", "arbitrary"),
        name="in_proj",
    )(x, norm_w, mod4, mod4, w_main, w_if)


def _outproj_kernel(a_ref, b_ref, c_ref, d_ref, w_ref, x_ref, g_ref, *rest, final):
    o_ref = rest[-1]
    for n in range(w_ref.shape[1] // GROUP_W):
        cols = slice(n * GROUP_W, (n + 1) * GROUP_W)
        acc = _dot(a_ref[...], w_ref[0:GROUP_W, cols])
        acc += _dot(b_ref[...], w_ref[GROUP_W:2 * GROUP_W, cols])
        acc += _dot(c_ref[...], w_ref[2 * GROUP_W:3 * GROUP_W, cols])
        acc += _dot(d_ref[...], w_ref[3 * GROUP_W:4 * GROUP_W, cols])
        o_ref[:, cols] = x_ref[:, cols] + g_ref[:, cols] * acc
    if final:
        fnw_ref = rest[0]
        for r in range(o_ref.shape[0] // ROW_CHUNK):
            rows = slice(r * ROW_CHUNK, (r + 1) * ROW_CHUNK)
            xn = o_ref[rows, :]
            o_ref[rows, :] = xn * lax.rsqrt(jnp.mean(xn * xn, axis=-1, keepdims=True) + NORM_EPS) * fnw_ref[...]


def _out_projection(branches, w_out, x, mod4, layer, mod_row0, rows_per_mod, final_norm_w):
    m, d = x.shape
    tm = 512
    per_batch = rows_per_mod < m
    if per_batch:
        gate_map = lambda i: (layer, mod_row0 + (i * tm) // rows_per_mod, 0, 2)
    else:
        gate_map = lambda i: (layer, mod_row0, 0, 2)
    final = final_norm_w is not None
    in_specs = [pl.BlockSpec((tm, GROUP_W), lambda i: (i, 0)) for _ in range(4)]
    in_specs += [pl.BlockSpec((d, d), lambda i: (0, 0)),
                 pl.BlockSpec((tm, d), lambda i: (i, 0)),
                 pl.BlockSpec((None, None, 1, d), gate_map)]
    args = list(branches) + [w_out, x, mod4]
    if final:
        in_specs.append(pl.BlockSpec((1, d), lambda i: (0, 0)))
        args.append(final_norm_w)
    return pl.pallas_call(
        functools.partial(_outproj_kernel, final=final),
        out_shape=jax.ShapeDtypeStruct((m, d), F32),
        grid=(m // tm,),
        in_specs=in_specs,
        out_specs=pl.BlockSpec((tm, d), lambda i: (i, 0)),
        compiler_params=_params("parallel"),
        name="out_proj",
    )(*args)


def _scan8(a, b, row, reverse):
    for dist in (1, 2, 4):
        if reverse:
            keep = row < 8 - dist
            shift = 8 - dist
        else:
            keep = row >= dist
            shift = dist
        a_sh = jnp.where(keep, pltpu.roll(a, shift, 0), 1.0)
        b_sh = jnp.where(keep, pltpu.roll(b, shift, 0), 0.0)
        b = b + a * b_sh
        a = a * a_sh
    return a, b


def _lru_kernel(x_ref, g_ref, cw_ref, cb_ref, wg_ref, br_ref, bi_ref, lam_ref, h0_ref,
                o_ref, st_ref, xc_s, pre_s, a_s, b_s, h_s, *, seq):
    w = GROUP_W
    rc = 256
    x = x_ref[...]
    t = lax.broadcasted_iota(jnp.int32, (seq, w), 0)
    xc = x * cw_ref[1:2, :] + cb_ref[...]
    xc += jnp.where(t >= 1, pltpu.roll(x, 1, 0), 0.0) * cw_ref[0:1, :]
    xc += jnp.where(t < seq - 1, pltpu.roll(x, seq - 1, 0), 0.0) * cw_ref[2:3, :]
    xc += jnp.where(t < seq - 2, pltpu.roll(x, seq - 2, 0), 0.0) * cw_ref[3:4, :]
    xc_s[...] = xc
    for g in range(HEADS):
        pre = _dot(xc_s[:, g * 128:(g + 1) * 128].astype(BF16), wg_ref[g])
        for k in range(4):
            pre_s[k, :, g * 128:(g + 1) * 128] = pre[:, k * 128:(k + 1) * 128]
    for d in range(2):
        log_lam = LRU_C * _log_sigmoid(lam_ref[d:d + 1, :])
        for c in range(seq // rc):
            rows = slice(c * rc, (c + 1) * rc)
            r = _sigmoid(pre_s[2 * d, rows, :] + br_ref[d:d + 1, :])
            i = _sigmoid(pre_s[2 * d + 1, rows, :] + bi_ref[d:d + 1, :])
            log_a = log_lam * r
            a = jnp.exp(log_a)
            a_s[d, rows, :] = a
            b_s[d, rows, :] = jnp.sqrt(-jnp.tanh(log_a) * (a * a + 1.0)) * i * xc_s[rows, :]

    row = lax.broadcasted_iota(jnp.int32, (8, w), 0)
    ngroups = seq // 8

    def body(g, carry):
        hf, hb = carry
        rf = pl.multiple_of(g * 8, 8)
        rb = pl.multiple_of((ngroups - 1 - g) * 8, 8)
        af, bf = _scan8(a_s[0, pl.ds(rf, 8), :], b_s[0, pl.ds(rf, 8), :], row, False)
        ab, bb = _scan8(a_s[1, pl.ds(rb, 8), :], b_s[1, pl.ds(rb, 8), :], row, True)
        hf_new = bf + af * hf
        hb_new = bb + ab * hb
        h_s[0, pl.ds(rf, 8), :] = hf_new
        h_s[1, pl.ds(rb, 8), :] = hb_new
        return (jnp.broadcast_to(hf_new[7:8, :], (8, w)), jnp.broadcast_to(hb_new[0:1, :], (8, w)))

    hf0 = jnp.broadcast_to(h0_ref[0:1, :], (8, w))
    hb0 = jnp.broadcast_to(h0_ref[1:2, :], (8, w))
    hf_last, hb_last = lax.fori_loop(0, ngroups, body, (hf0, hb0))
    st_ref[0:1, :] = hf_last[0:1, :]
    st_ref[1:2, :] = hb_last[0:1, :]
    for c in range(seq // rc):
        rows = slice(c * rc, (c + 1) * rc)
        o_ref[rows, :] = ((h_s[0, rows, :] + h_s[1, rows, :]) * _silu(g_ref[rows, :])).astype(BF16)


def _lru(p_main, batch, seq, conv_w, conv_b, w_gates, b_r, b_i, lam, h0):
    w = GROUP_W
    const = lambda *shape: pl.BlockSpec(shape, lambda b: (0,) * len(shape))
    return pl.pallas_call(
        functools.partial(_lru_kernel, seq=seq),
        out_shape=(jax.ShapeDtypeStruct((batch * seq, w), BF16), jax.ShapeDtypeStruct((batch, 2, w), F32)),
        grid=(batch,),
        in_specs=[pl.BlockSpec((seq, w), lambda b: (b, COL_LRU_X)),
                  pl.BlockSpec((seq, w), lambda b: (b, COL_LRU_G)),
                  const(4, w), const(1, w), const(HEADS, 128, 4 * 128), const(2, w), const(2, w), const(2, w),
                  pl.BlockSpec((None, 2, w), lambda b: (b, 0, 0))],
        out_specs=(pl.BlockSpec((seq, w), lambda b: (b, 0)),
                   pl.BlockSpec((None, 2, w), lambda b: (b, 0, 0))),
        scratch_shapes=[pltpu.VMEM((seq, w), F32), pltpu.VMEM((4, seq, w), F32),
                        pltpu.VMEM((2, seq, w), F32), pltpu.VMEM((2, seq, w), F32), pltpu.VMEM((2, seq, w), F32)],
        compiler_params=_params("parallel"),
        name="rg_lru",
    )(p_main, p_main, conv_w, conv_b, w_gates, b_r, b_i, lam, h0)


def _head_rms(x, w):
    return x * lax.rsqrt(jnp.mean(x * x, axis=-1, keepdims=True) + NORM_EPS) * w


def _rope(x, cos, sin_lo, sin_hi):
    return x * cos + pltpu.roll(x, 96, 1) * sin_lo + pltpu.roll(x, 32, 1) * sin_hi


def _attn_kernel(*refs, n_kv, has_cache, has_bias, qk_norm, rope, write_kv):
    it = iter(refs)
    q_ref, k_ref, v_ref, g_ref = next(it), next(it), next(it), next(it)
    kc_ref = vc_ref = bias_ref = qn_ref = kn_ref = None
    if has_cache:
        kc_ref, vc_ref = next(it), next(it)
    if has_bias:
        bias_ref = next(it)
    if qk_norm:
        qn_ref, kn_ref = next(it), next(it)
    if rope:
        cq_ref, slq_ref, shq_ref, ck_ref, slk_ref, shk_ref = (next(it) for _ in range(6))
    o_ref = next(it)
    if write_kv:
        ko_ref, vo_ref = next(it), next(it)

    scale = HEAD_DIM ** -0.5
    group = HEADS // n_kv
    for kvh in range(n_kv):
        cols = slice(kvh * HEAD_DIM, (kvh + 1) * HEAD_DIM)
        k = k_ref[:, cols]
        v = v_ref[:, cols]
        if qk_norm:
            k = _head_rms(k, kn_ref[...])
        if write_kv:
            ko_ref[:, cols] = k
            vo_ref[:, cols] = v
        if rope:
            k = _rope(k, ck_ref[...], slk_ref[...], shk_ref[...])
        kb = k.astype(BF16)
        vb = v.astype(BF16)
        if has_cache:
            kcb = kc_ref[:, cols].astype(BF16)
            vcb = vc_ref[:, cols].astype(BF16)
        for gi in range(group):
            h = kvh * group + gi
            hc = slice(h * HEAD_DIM, (h + 1) * HEAD_DIM)
            q = q_ref[:, hc]
            if qk_norm:
                q = _head_rms(q, qn_ref[...])
            if rope:
                q = _rope(q, cq_ref[...], slq_ref[...], shq_ref[...])
            qb = q.astype(BF16)
            s1 = _dot_nt(qb, kb) * scale
            if has_bias:
                s1 = s1 + bias_ref[h]
            mx = jnp.max(s1, axis=-1, keepdims=True)
            if has_cache:
                s2 = _dot_nt(qb, kcb) * scale
                mx = jnp.maximum(mx, jnp.max(s2, axis=-1, keepdims=True))
            p1 = jnp.exp(s1 - mx)
            den = jnp.sum(p1, axis=-1, keepdims=True)
            acc = _dot(p1.astype(BF16), vb)
            if has_cache:
                p2 = jnp.exp(s2 - mx)
                den = den + jnp.sum(p2, axis=-1, keepdims=True)
                acc = acc + _dot(p2.astype(BF16), vcb)
            o_ref[:, hc] = ((acc / den) * _silu(g_ref[:, hc])).astype(BF16)


def _attention(p_main, batch, seq, *, col_q, col_g, k_spec, v_spec, n_kv, cache=None, bias=None,
               qk_norm=None, rope=None, write_kv=False):
    tq = 256
    nq = seq // tq
    kvw = n_kv * HEAD_DIM
    in_specs = [pl.BlockSpec((tq, GROUP_W), lambda b, i: (b * nq + i, col_q)),
                pl.BlockSpec((seq, kvw), lambda b, i: (b, k_spec)),
                pl.BlockSpec((seq, kvw), lambda b, i: (b, v_spec)),
                pl.BlockSpec((tq, GROUP_W), lambda b, i: (b * nq + i, col_g))]
    args = [p_main, p_main, p_main, p_main]
    if cache is not None:
        kc, vc, layer = cache
        lc = kc.shape[2]
        in_specs += [pl.BlockSpec((None, None, lc, kvw), lambda b, i: (b, layer, 0, 0))] * 2
        args += [kc, vc]
    if bias is not None:
        in_specs.append(pl.BlockSpec((HEADS, tq, seq), lambda b, i: (0, i, 0)))
        args.append(bias)
    if qk_norm is not None:
        in_specs += [pl.BlockSpec((1, HEAD_DIM), lambda b, i: (0, 0))] * 2
        args += list(qk_norm)
    if rope is not None:
        in_specs += [pl.BlockSpec((tq, HEAD_DIM), lambda b, i: (i, 0))] * 3
        in_specs += [pl.BlockSpec((seq, HEAD_DIM), lambda b, i: (0, 0))] * 3
        args += list(rope) + list(rope)
    out_shape = [jax.ShapeDtypeStruct((batch * seq, GROUP_W), BF16)]
    out_specs = [pl.BlockSpec((tq, GROUP_W), lambda b, i: (b * nq + i, 0))]
    if write_kv:
        assert nq == 1
        out_shape += [jax.ShapeDtypeStruct((batch * seq, kvw), F32)] * 2
        out_specs += [pl.BlockSpec((seq, kvw), lambda b, i: (b, 0))] * 2
    res = pl.pallas_call(
        functools.partial(_attn_kernel, n_kv=n_kv, has_cache=cache is not None, has_bias=bias is not None,
                          qk_norm=qk_norm is not None, rope=rope is not None, write_kv=write_kv),
        out_shape=tuple(out_shape),
        grid=(batch, nq),
        in_specs=in_specs,
        out_specs=tuple(out_specs),
        compiler_params=_params("parallel", "arbitrary"),
        name="attention",
    )(*args)
    return res


def _mlstm_kernel(*refs, seq, has_state, write_state):
    it = iter(refs)
    q_ref, k_ref, v_ref, og_ref, g_ref, gc_ref, gr_ref, bc_ref, br_ref, onw_ref = (next(it) for _ in range(10))
    if has_state:
        c0_ref, n0_ref, m0_ref = next(it), next(it), next(it)
    out_ref = next(it)
    if write_state:
        cout_ref, nout_ref, mout_ref = next(it), next(it), next(it)
    hsum_s, c_s = next(it), next(it)

    L = ML_CHUNK
    nc = seq // L
    scale = HEAD_DIM ** -0.5
    gc = gc_ref[...] + bc_ref[...]
    gr = gr_ref[...] + br_ref[...]
    is_f_col = (lax.broadcasted_iota(jnp.int32, gc.shape, 1) % 2) == 1
    is_f_row = (lax.broadcasted_iota(jnp.int32, gr.shape, 0) % 2) == 1
    gc = jnp.where(is_f_col, _log_sigmoid(gc), gc)
    gr = jnp.where(is_f_row, _log_sigmoid(gr), gr)
    ti = lax.broadcasted_iota(jnp.int32, (L, L), 0)
    si = lax.broadcasted_iota(jnp.int32, (L, L), 1)

    for d in range(2):
        mask = (si <= ti) if d == 0 else (si >= ti)
        ones = jnp.where(mask, 1.0, 0.0).astype(BF16)
        if has_state:
            c_s[...] = c0_ref[d]
            n = n0_ref[d:d + 1, :]
            m = m0_ref[d:d + 1, 0:1]
        else:
            c_s[...] = jnp.zeros((HEAD_DIM, HEAD_DIM), F32)
            n = jnp.zeros((1, HEAD_DIM), F32)
            m = jnp.zeros((1, 1), F32)
        for step in range(nc):
            c = step if d == 0 else nc - 1 - step
            rows = slice(c * L, (c + 1) * L)
            qs = q_ref[rows, :] * scale
            k = k_ref[rows, :]
            qb, kb, vb = qs.astype(BF16), k.astype(BF16), v_ref[rows, :].astype(BF16)
            ii_c, ff_c = gc[rows, 2 * d:2 * d + 1], gc[rows, 2 * d + 1:2 * d + 2]
            ii_r, ff_r = gr[2 * d:2 * d + 1, rows], gr[2 * d + 1:2 * d + 2, rows]
            b_c = sum(_dot(ones, p) for p in _split3(jnp.broadcast_to(ff_c, (L, HEAD_DIM))))[:, 0:1]
            b_r = sum(_dot_nt(p, ones) for p in _split3(jnp.broadcast_to(ff_r, (8, L))))[0:1, :]
            log_w = jnp.where(mask, b_c - b_r + ii_r, NEG_INF)
            m_inter = b_c + m
            m_t = jnp.maximum(m_inter, jnp.max(log_w, axis=-1, keepdims=True))
            w_inter = jnp.exp(m_inter - m_t)
            scores = _dot_nt(qb, kb) * jnp.exp(log_w - m_t)
            cmat = c_s[...]
            num = _dot(scores.astype(BF16), vb) + w_inter * _dot(qb, cmat.astype(BF16))
            den = jnp.sum(scores, axis=-1, keepdims=True) + w_inter * jnp.sum(qs * n, axis=-1, keepdims=True)
            h = num / jnp.maximum(jnp.abs(den), jnp.exp(-m_t))
            if d == 0:
                hsum_s[rows, :] = h
            else:
                hsum_s[rows, :] += h
            b_last = b_c[L - 1:L, :] if d == 0 else b_c[0:1, :]
            m_new = jnp.maximum(b_last + m, jnp.max(b_last - b_r + ii_r, axis=-1, keepdims=True))
            w_prev = jnp.exp(b_last + m - m_new)
            ku = k * jnp.exp(b_last - b_c + ii_c - m_new)
            c_s[...] = w_prev * cmat + _dot_tn(ku.astype(BF16), vb)
            n = w_prev * n + jnp.sum(ku, axis=0, keepdims=True)
            m = m_new
        if write_state:
            cout_ref[d] = c_s[...]
            nout_ref[d:d + 1, :] = n
            mout_ref[d:d + 1, :] = jnp.broadcast_to(m, (1, HEAD_DIM))

    hs = hsum_s[...]
    y = hs * lax.rsqrt(jnp.mean(hs * hs, axis=-1, keepdims=True) + NORM_EPS) * onw_ref[...]
    out_ref[...] = (y * _sigmoid(og_ref[...]) * _silu(g_ref[...])).astype(BF16)


def _mlstm(p_main, p_if, batch, seq, gate_b, out_norm_w, state=None, write_state=False):
    hd = HEAD_DIM
    gates = p_if[:, :4 * HEADS].reshape(batch, seq, 4, HEADS)
    gcol = jnp.transpose(gates, (0, 3, 1, 2))
    grow = jnp.transpose(gates, (0, 3, 2, 1))
    gb = gate_b.reshape(4, HEADS).T
    col = lambda base: (lambda b, h: (b, base * HEADS + h))
    in_specs = [pl.BlockSpec((seq, hd), col(COL_ML_Q)), pl.BlockSpec((seq, hd), col(COL_ML_K)),
                pl.BlockSpec((seq, hd), col(COL_ML_V)), pl.BlockSpec((seq, hd), col(COL_ML_O)),
                pl.BlockSpec((seq, hd), col(COL_ML_G)),
                pl.BlockSpec((None, None, seq, 4), lambda b, h: (b, h, 0, 0)),
                pl.BlockSpec((None, None, 4, seq), lambda b, h: (b, h, 0, 0)),
                pl.BlockSpec((None, 1, 4), lambda b, h: (h, 0, 0)),
                pl.BlockSpec((None, 4, 1), lambda b, h: (h, 0, 0)),
                pl.BlockSpec((None, 1, hd), lambda b, h: (h, 0, 0))]
    args = [p_main] * 5 + [gcol, grow, gb.reshape(HEADS, 1, 4), gb.reshape(HEADS, 4, 1),
                           out_norm_w.reshape(HEADS, 1, hd)]
    if state is not None:
        c0, n0, m0, layer = state
        n0t = jnp.transpose(n0[:, layer], (0, 2, 1, 3))
        m0t = jnp.broadcast_to(jnp.transpose(m0[:, layer], (0, 2, 1))[..., None], (batch, HEADS, 2, hd))
        in_specs += [pl.BlockSpec((None, None, 2, None, hd, hd), lambda b, h: (b, layer, 0, h, 0, 0)),
                     pl.BlockSpec((None, None, 2, hd), lambda b, h: (b, h, 0, 0)),
                     pl.BlockSpec((None, None, 2, hd), lambda b, h: (b, h, 0, 0))]
        args += [c0, n0t, m0t]
    out_shape = [jax.ShapeDtypeStruct((batch * seq, GROUP_W), BF16)]
    out_specs = [pl.BlockSpec((seq, hd), lambda b, h: (b, h))]
    if write_state:
        out_shape += [jax.ShapeDtypeStruct((batch, 2, HEADS, hd, hd), F32),
                      jax.ShapeDtypeStruct((batch, HEADS, 2, hd), F32),
                      jax.ShapeDtypeStruct((batch, HEADS, 2, hd), F32)]
        out_specs += [pl.BlockSpec((None, 2, None, hd, hd), lambda b, h: (b, 0, h, 0, 0)),
                      pl.BlockSpec((None, None, 2, hd), lambda b, h: (b, h, 0, 0)),
                      pl.BlockSpec((None, None, 2, hd), lambda b, h: (b, h, 0, 0))]
    return pl.pallas_call(
        functools.partial(_mlstm_kernel, seq=seq, has_state=state is not None, write_state=write_state),
        out_shape=tuple(out_shape),
        grid=(batch, HEADS),
        in_specs=in_specs,
        out_specs=tuple(out_specs),
        scratch_shapes=[pltpu.VMEM((seq, hd), F32), pltpu.VMEM((hd, hd), F32)],
        compiler_params=_params("parallel", "parallel"),
        name="mlstm",
    )(*args)


def _na_bias(rpb, seq):
    rows = seq // GRID_W
    wr = min(NA_ROWS, rows)
    r = np.arange(rows)
    c = np.arange(GRID_W)
    r_start = np.clip(r - wr // 2, 0, rows - wr)
    c_start = np.clip(c - NA_COLS // 2, 0, GRID_W - NA_COLS)
    row_ok = (r[None, :] >= r_start[:, None]) & (r[None, :] < r_start[:, None] + wr)
    col_ok = (c[None, :] >= c_start[:, None]) & (c[None, :] < c_start[:, None] + NA_COLS)
    row_off = np.clip(r[None, :] - r[:, None] + NA_ROWS - 1, 0, 2 * NA_ROWS - 2)
    col_off = np.clip(c[None, :] - c[:, None] + NA_COLS - 1, 0, 2 * NA_COLS - 2)
    row_sel = (row_off[..., None] == np.arange(2 * NA_ROWS - 1)).astype(np.float32)
    col_sel = (col_off[..., None] == np.arange(2 * NA_COLS - 1)).astype(np.float32)
    dense = jnp.einsum('abi,lhij,cdj->lhacbd', row_sel, rpb.astype(F32), col_sel,
                       precision=lax.Precision.HIGHEST)
    valid = (row_ok[:, None, :, None] & col_ok[None, :, None, :])
    dense = jnp.where(valid, dense, NEG_INF)
    return dense.reshape(rpb.shape[0], rpb.shape[1], seq, seq)


def _rope_tables(seq):
    t = jnp.arange(seq)
    nf = HEAD_DIM // 4
    inv = ROPE_THETA ** (-jnp.arange(nf, dtype=F32) / nf)
    zeros = jnp.zeros((seq, nf), F32)

    def half(pos):
        ang = pos.astype(F32)[:, None] * inv
        cs, sn = jnp.cos(ang), jnp.sin(ang)
        return jnp.concatenate([cs, cs], -1), jnp.concatenate([-sn, zeros], -1), jnp.concatenate([zeros, sn], -1)

    parts = [half(t // GRID_W), half(t % GRID_W)]
    return tuple(jnp.concatenate([parts[0][i], parts[1][i]], -1) for i in range(3))


def kernel(x_prompt, x_sample, cache_na_k, cache_na_v, cache_gqa_k, cache_gqa_v, state_lru, state_mlstm_C,
           state_mlstm_n, state_mlstm_m, c, c_ctx, norm_w, ada_w, ada_b, w_in, lru_conv_w, lru_conv_b, lru_wr,
           lru_br, lru_wi, lru_bi, lru_lambda, na_rpb, gqa_qnorm, gqa_knorm, ml_gate_b, ml_out_norm, w_out,
           final_norm_w):
    bp, tp, d = x_prompt.shape
    bs, ts, _ = x_sample.shape
    depth = w_in.shape[0]
    past = cache_na_k.shape[2]

    cond = jnp.zeros((COND_ROWS, d), F32).at[0].set(c_ctx).at[1:1 + bs].set(c)
    mod4 = _modulation(cond, ada_w, ada_b).reshape(depth, COND_ROWS, 1, 3 * d)

    w_main = w_in[:, :, :MAIN_W].astype(BF16)
    w_if = jnp.pad(w_in[:, :, MAIN_W:], ((0, 0), (0, 0), (0, 128 - (w_in.shape[2] - MAIN_W)))).astype(BF16)
    w_out_b = w_out.astype(BF16)
    w_gates = jnp.concatenate([lru_wr[:, 0], lru_wi[:, 0], lru_wr[:, 1], lru_wi[:, 1]], axis=-1).astype(BF16)
    bias_all = _na_bias(na_rpb, ts)
    rope = _rope_tables(ts)
    ck_na = cache_na_k.reshape(bs, depth, past, GROUP_W)
    cv_na = cache_na_v.reshape(bs, depth, past, GROUP_W)
    kvw = GQA_KV_HEADS * HEAD_DIM
    ck_gqa = cache_gqa_k.reshape(bs, depth, past, kvw)
    cv_gqa = cache_gqa_v.reshape(bs, depth, past, kvw)
    zeros_h0 = jnp.zeros((bp, 2, GROUP_W), F32)

    xp = x_prompt.reshape(bp * tp, d)
    xs = x_sample.reshape(bs * ts, d)
    norm_w2 = norm_w.reshape(depth, 1, d)
    fnw = final_norm_w.reshape(1, d)
    new = {k: [] for k in ('na_k', 'na_v', 'gqa_k', 'gqa_v', 'lru', 'C', 'n', 'm')}

    for l in range(depth):
        last = l == depth - 1
        lru_args = (lru_conv_w[l], lru_conv_b[l].reshape(1, GROUP_W), w_gates[l], lru_br[l], lru_bi[l], lru_lambda[l])
        qk_norm = (gqa_qnorm[l].reshape(1, HEAD_DIM), gqa_knorm[l].reshape(1, HEAD_DIM))

        p, pif = _in_projection(xp, norm_w2[l], mod4, l, 0, bp * tp, w_main[l], w_if[l])
        out_a, st_lru = _lru(p, bp, tp, *lru_args, zeros_h0)
        out_b, na_k, na_v = _attention(p, bp, tp, col_q=COL_NA_Q, col_g=COL_NA_G, k_spec=COL_NA_K,
                                       v_spec=COL_NA_V, n_kv=HEADS, write_kv=True)
        out_c, gqa_k, gqa_v = _attention(p, bp, tp, col_q=COL_GQA_Q, col_g=COL_GQA_G, k_spec=2 * COL_GQA_KV,
                                         v_spec=2 * COL_GQA_KV + 1, n_kv=GQA_KV_HEADS, qk_norm=qk_norm,
                                         write_kv=True)
        out_d, st_c, st_n, st_m = _mlstm(p, pif, bp, tp, ml_gate_b[l], ml_out_norm[l], write_state=True)
        xp = _out_projection((out_a, out_b, out_c, out_d), w_out_b[l], xp, mod4, l, 0, bp * tp,
                             fnw if last else None)
        new['na_k'].append(na_k.reshape(bp, tp, HEADS, HEAD_DIM))
        new['na_v'].append(na_v.reshape(bp, tp, HEADS, HEAD_DIM))
        new['gqa_k'].append(gqa_k.reshape(bp, tp, GQA_KV_HEADS, HEAD_DIM))
        new['gqa_v'].append(gqa_v.reshape(bp, tp, GQA_KV_HEADS, HEAD_DIM))
        new['lru'].append(st_lru)
        new['C'].append(st_c)
        new['n'].append(jnp.transpose(st_n, (0, 2, 1, 3)))
        new['m'].append(jnp.transpose(st_m[..., 0], (0, 2, 1)))

        p, pif = _in_projection(xs, norm_w2[l], mod4, l, 1, ts, w_main[l], w_if[l])
        out_a, _ = _lru(p, bs, ts, *lru_args, state_lru[:, l])
        (out_b,) = _attention(p, bs, ts, col_q=COL_NA_Q, col_g=COL_NA_G, k_spec=COL_NA_K, v_spec=COL_NA_V,
                              n_kv=HEADS, cache=(ck_na, cv_na, l), bias=bias_all[l])
        (out_c,) = _attention(p, bs, ts, col_q=COL_GQA_Q, col_g=COL_GQA_G, k_spec=2 * COL_GQA_KV,
                              v_spec=2 * COL_GQA_KV + 1, n_kv=GQA_KV_HEADS, cache=(ck_gqa, cv_gqa, l),
                              qk_norm=qk_norm, rope=rope)
        (out_d,) = _mlstm(p, pif, bs, ts, ml_gate_b[l], ml_out_norm[l],
                          state=(state_mlstm_C, state_mlstm_n, state_mlstm_m, l))
        xs = _out_projection((out_a, out_b, out_c, out_d), w_out_b[l], xs, mod4, l, 1, ts,
                             fnw if last else None)

    stacked = {k: jnp.stack(v, axis=1) for k, v in new.items()}
    return (xp.reshape(bp, tp, d), xs.reshape(bs, ts, d), stacked['na_k'], stacked['na_v'], stacked['gqa_k'],
            stacked['gqa_v'], stacked['lru'], stacked['C'], stacked['n'], stacked['m'])
```

```python
import functools

import numpy as np
import jax
import jax.numpy as jnp
from jax import lax
from jax.experimental import pallas as pl
from jax.experimental.pallas import tpu as pltpu

F32 = jnp.float32
BF16 = jnp.bfloat16

HEAD_DIM = 128
GROUP_W = 512
HEADS = GROUP_W // HEAD_DIM
GQA_KV_HEADS = 2
GRID_W = 64
NORM_EPS = 1e-6
NEG_INF = -1e30
LRU_C = 8.0
NA_ROWS, NA_COLS = 8, 16
ROPE_THETA = 10000.0
ML_CHUNK = 256
COND_ROWS = 16
ROW_CHUNK = 256
VMEM_LIMIT = 52 * 1024 * 1024

COL_LRU_X, COL_LRU_G, COL_NA_Q, COL_NA_K, COL_NA_V, COL_NA_G = 0, 1, 2, 3, 4, 5
COL_GQA_Q, COL_GQA_KV, COL_GQA_G = 6, 7, 8
COL_ML_Q, COL_ML_K, COL_ML_V, COL_ML_O, COL_ML_G = 9, 10, 11, 12, 13
MAIN_W = 14 * GROUP_W


def _params(*sem):
    return pltpu.CompilerParams(dimension_semantics=sem, vmem_limit_bytes=VMEM_LIMIT)


def _sigmoid(x):
    return 1.0 / (1.0 + jnp.exp(-x))


def _silu(x):
    return x * _sigmoid(x)


def _log_sigmoid(x):
    return jnp.minimum(x, 0.0) - jnp.log1p(jnp.exp(-jnp.abs(x)))


def _dot(a, b):
    return jnp.dot(a, b, preferred_element_type=F32)


def _dot_nt(a, b):
    return lax.dot_general(a, b, (((1,), (1,)), ((), ())), preferred_element_type=F32)


def _dot_tn(a, b):
    return lax.dot_general(a, b, (((0,), (0,)), ((), ())), preferred_element_type=F32)


def _split3(x):
    x1 = x.astype(BF16)
    r = x - x1.astype(F32)
    x2 = r.astype(BF16)
    x3 = (r - x2.astype(F32)).astype(BF16)
    return x1, x2, x3


def _mod_kernel(cond_ref, w_ref, b_ref, o_ref):
    s = _silu(cond_ref[...]).astype(BF16)
    o_ref[...] = _dot(s, w_ref[...].astype(BF16)) + b_ref[...]


def _modulation(cond, ada_w, ada_b):
    depth, d, n = ada_w.shape
    tn = 1024
    return pl.pallas_call(
        _mod_kernel,
        out_shape=jax.ShapeDtypeStruct((depth, COND_ROWS, n), F32),
        grid=(depth, n // tn),
        in_specs=[pl.BlockSpec((COND_ROWS, d), lambda l, j: (0, 0)),
                  pl.BlockSpec((None, d, tn), lambda l, j: (l, 0, j)),
                  pl.BlockSpec((None, 1, tn), lambda l, j: (l, 0, j))],
        out_specs=pl.BlockSpec((None, COND_ROWS, tn), lambda l, j: (l, 0, j)),
        compiler_params=_params("parallel", "parallel"),
        name="adaln_mod",
    )(cond, ada_w, ada_b.reshape(depth, 1, n))


def _inproj_kernel(x_ref, nw_ref, sh_ref, sc_ref, w_ref, wif_ref, o_ref, oif_ref, xm_ref):
    @pl.when(pl.program_id(1) == 0)
    def _():
        for r in range(x_ref.shape[0] // ROW_CHUNK):
            rows = slice(r * ROW_CHUNK, (r + 1) * ROW_CHUNK)
            x = x_ref[rows, :]
            y = x * lax.rsqrt(jnp.mean(x * x, axis=-1, keepdims=True) + NORM_EPS) * nw_ref[...]
            xm_ref[rows, :] = (y * (1.0 + sc_ref[...]) + sh_ref[...]).astype(BF16)
        oif_ref[...] = _dot(xm_ref[...], wif_ref[...])

    o_ref[...] = _dot(xm_ref[...], w_ref[...])


def _in_projection(x, norm_w, mod4, layer, mod_row0, rows_per_mod, w_main, w_if):
    m, d = x.shape
    tm, tn = min(1024, m), 1024
    assert m % tm == 0 and rows_per_mod % tm == 0
    per_batch = rows_per_mod < m

    def mod_map(col):
        if per_batch:
            return lambda i, j: (layer, mod_row0 + (i * tm) // rows_per_mod, 0, col)
        return lambda i, j: (layer, mod_row0, 0, col)

    return pl.pallas_call(
        _inproj_kernel,
        out_shape=(jax.ShapeDtypeStruct((m, MAIN_W), F32), jax.ShapeDtypeStruct((m, 128), F32)),
        grid=(m // tm, MAIN_W // tn),
        in_specs=[pl.BlockSpec((tm, d), lambda i, j: (i, 0)),
                  pl.BlockSpec((1, d), lambda i, j: (0, 0)),
                  pl.BlockSpec((None, None, 1, d), mod_map(0)),
                  pl.BlockSpec((None, None, 1, d), mod_map(1)),
                  pl.BlockSpec((d, tn), lambda i, j: (0, j)),
                  pl.BlockSpec((d, 128), lambda i, j: (0, 0))],
        out_specs=(pl.BlockSpec((tm, tn), lambda i, j: (i, j)),
                   pl.BlockSpec((tm, 128), lambda i, j: (i, 0))),
        scratch_shapes=[pltpu.VMEM((tm, d), BF16)],
        compiler_params=_params("parallel", "arbitrary"),
        name="in_proj",
    )(x, norm_w, mod4, mod4, w_main, w_if)


def _outproj_kernel(a_ref, b_ref, c_ref, d_ref, w_ref, x_ref, g_ref, *rest, final):
    o_ref = rest[-1]
    for n in range(w_ref.shape[1] // GROUP_W):
        cols = slice(n * GROUP_W, (n + 1) * GROUP_W)
        acc = _dot(a_ref[...], w_ref[0:GROUP_W, cols])
        acc += _dot(b_ref[...], w_ref[GROUP_W:2 * GROUP_W, cols])
        acc += _dot(c_ref[...], w_ref[2 * GROUP_W:3 * GROUP_W, cols])
        acc += _dot(d_ref[...], w_ref[3 * GROUP_W:4 * GROUP_W, cols])
        o_ref[:, cols] = x_ref[:, cols] + g_ref[:, cols] * acc
    if final:
        fnw_ref = rest[0]
        for r in range(o_ref.shape[0] // ROW_CHUNK):
            rows = slice(r * ROW_CHUNK, (r + 1) * ROW_CHUNK)
            xn = o_ref[rows, :]
            o_ref[rows, :] = xn * lax.rsqrt(jnp.mean(xn * xn, axis=-1, keepdims=True) + NORM_EPS) * fnw_ref[...]


def _out_projection(branches, w_out, x, mod4, layer, mod_row0, rows_per_mod, final_norm_w):
    m, d = x.shape
    tm = 512
    per_batch = rows_per_mod < m
    if per_batch:
        gate_map = lambda i: (layer, mod_row0 + (i * tm) // rows_per_mod, 0, 2)
    else:
        gate_map = lambda i: (layer, mod_row0, 0, 2)
    final = final_norm_w is not None
    in_specs = [pl.BlockSpec((tm, GROUP_W), lambda i: (i, 0)) for _ in range(4)]
    in_specs += [pl.BlockSpec((d, d), lambda i: (0, 0)),
                 pl.BlockSpec((tm, d), lambda i: (i, 0)),
                 pl.BlockSpec((None, None, 1, d), gate_map)]
    args = list(branches) + [w_out, x, mod4]
    if final:
        in_specs.append(pl.BlockSpec((1, d), lambda i: (0, 0)))
        args.append(final_norm_w)
    return pl.pallas_call(
        functools.partial(_outproj_kernel, final=final),
        out_shape=jax.ShapeDtypeStruct((m, d), F32),
        grid=(m // tm,),
        in_specs=in_specs,
        out_specs=pl.BlockSpec((tm, d), lambda i: (i, 0)),
        compiler_params=_params("parallel"),
        name="out_proj",
    )(*args)


def _scan8(a, b, row, reverse):
    for dist in (1, 2, 4):
        if reverse:
            keep = row < 8 - dist
            shift = 8 - dist
        else:
            keep = row >= dist
            shift = dist
        a_sh = jnp.where(keep, pltpu.roll(a, shift, 0), 1.0)
        b_sh = jnp.where(keep, pltpu.roll(b, shift, 0), 0.0)
        b = b + a * b_sh
        a = a * a_sh
    return a, b


def _lru_kernel(x_ref, g_ref, cw_ref, cb_ref, wg_ref, br_ref, bi_ref, lam_ref, h0_ref,
                o_ref, st_ref, xc_s, pre_s, a_s, b_s, h_s, *, seq):
    w = GROUP_W
    rc = 256
    x = x_ref[...]
    t = lax.broadcasted_iota(jnp.int32, (seq, w), 0)
    xc = x * cw_ref[1:2, :] + cb_ref[...]
    xc += jnp.where(t >= 1, pltpu.roll(x, 1, 0), 0.0) * cw_ref[0:1, :]
    xc += jnp.where(t < seq - 1, pltpu.roll(x, seq - 1, 0), 0.0) * cw_ref[2:3, :]
    xc += jnp.where(t < seq - 2, pltpu.roll(x, seq - 2, 0), 0.0) * cw_ref[3:4, :]
    xc_s[...] = xc
    for g in range(HEADS):
        pre = _dot(xc_s[:, g * 128:(g + 1) * 128].astype(BF16), wg_ref[g])
        for k in range(4):
            pre_s[k, :, g * 128:(g + 1) * 128] = pre[:, k * 128:(k + 1) * 128]
    for d in range(2):
        log_lam = LRU_C * _log_sigmoid(lam_ref[d:d + 1, :])
        for c in range(seq // rc):
            rows = slice(c * rc, (c + 1) * rc)
            r = _sigmoid(pre_s[2 * d, rows, :] + br_ref[d:d + 1, :])
            i = _sigmoid(pre_s[2 * d + 1, rows, :] + bi_ref[d:d + 1, :])
            log_a = log_lam * r
            a = jnp.exp(log_a)
            a_s[d, rows, :] = a
            b_s[d, rows, :] = jnp.sqrt(-jnp.tanh(log_a) * (a * a + 1.0)) * i * xc_s[rows, :]

    row = lax.broadcasted_iota(jnp.int32, (8, w), 0)
    ngroups = seq // 8

    def body(g, carry):
        hf, hb = carry
        rf = pl.multiple_of(g * 8, 8)
        rb = pl.multiple_of((ngroups - 1 - g) * 8, 8)
        af, bf = _scan8(a_s[0, pl.ds(rf, 8), :], b_s[0, pl.ds(rf, 8), :], row, False)
        ab, bb = _scan8(a_s[1, pl.ds(rb, 8), :], b_s[1, pl.ds(rb, 8), :], row, True)
        hf_new = bf + af * hf
        hb_new = bb + ab * hb
        h_s[0, pl.ds(rf, 8), :] = hf_new
        h_s[1, pl.ds(rb, 8), :] = hb_new
        return (jnp.broadcast_to(hf_new[7:8, :], (8, w)), jnp.broadcast_to(hb_new[0:1, :], (8, w)))

    hf0 = jnp.broadcast_to(h0_ref[0:1, :], (8, w))
    hb0 = jnp.broadcast_to(h0_ref[1:2, :], (8, w))
    hf_last, hb_last = lax.fori_loop(0, ngroups, body, (hf0, hb0))
    st_ref[0:1, :] = hf_last[0:1, :]
    st_ref[1:2, :] = hb_last[0:1, :]
    for c in range(seq // rc):
        rows = slice(c * rc, (c + 1) * rc)
        o_ref[rows, :] = ((h_s[0, rows, :] + h_s[1, rows, :]) * _silu(g_ref[rows, :])).astype(BF16)


def _lru(p_main, batch, seq, conv_w, conv_b, w_gates, b_r, b_i, lam, h0):
    w = GROUP_W
    const = lambda *shape: pl.BlockSpec(shape, lambda b: (0,) * len(shape))
    return pl.pallas_call(
        functools.partial(_lru_kernel, seq=seq),
        out_shape=(jax.ShapeDtypeStruct((batch * seq, w), BF16), jax.ShapeDtypeStruct((batch, 2, w), F32)),
        grid=(batch,),
        in_specs=[pl.BlockSpec((seq, w), lambda b: (b, COL_LRU_X)),
                  pl.BlockSpec((seq, w), lambda b: (b, COL_LRU_G)),
                  const(4, w), const(1, w), const(HEADS, 128, 4 * 128), const(2, w), const(2, w), const(2, w),
                  pl.BlockSpec((None, 2, w), lambda b: (b, 0, 0))],
        out_specs=(pl.BlockSpec((seq, w), lambda b: (b, 0)),
                   pl.BlockSpec((None, 2, w), lambda b: (b, 0, 0))),
        scratch_shapes=[pltpu.VMEM((seq, w), F32), pltpu.VMEM((4, seq, w), F32),
                        pltpu.VMEM((2, seq, w), F32), pltpu.VMEM((2, seq, w), F32), pltpu.VMEM((2, seq, w), F32)],
        compiler_params=_params("parallel"),
        name="rg_lru",
    )(p_main, p_main, conv_w, conv_b, w_gates, b_r, b_i, lam, h0)


def _head_rms(x, w):
    return x * lax.rsqrt(jnp.mean(x * x, axis=-1, keepdims=True) + NORM_EPS) * w


def _rope(x, cos, sin_lo, sin_hi):
    return x * cos + pltpu.roll(x, 96, 1) * sin_lo + pltpu.roll(x, 32, 1) * sin_hi


def _attn_kernel(*refs, n_kv, has_cache, has_bias, qk_norm, rope, write_kv, key_window):
    it = iter(refs)
    q_ref, k_ref, v_ref, g_ref = next(it), next(it), next(it), next(it)
    kc_ref = vc_ref = bias_ref = qn_ref = kn_ref = None
    if has_cache:
        kc_ref, vc_ref = next(it), next(it)
    if has_bias:
        bias_ref = next(it)
    if qk_norm:
        qn_ref, kn_ref = next(it), next(it)
    if rope:
        cq_ref, slq_ref, shq_ref, ck_ref, slk_ref, shk_ref = (next(it) for _ in range(6))
    o_ref = next(it)
    if write_kv:
        ko_ref, vo_ref = next(it), next(it)
    kb_s, vb_s = next(it), next(it)
    if has_cache:
        kcb_s, vcb_s = next(it), next(it)

    @pl.when(pl.program_id(1) == 0)
    def _():
        for kvh in range(n_kv):
            cols = slice(kvh * HEAD_DIM, (kvh + 1) * HEAD_DIM)
            k = k_ref[:, cols]
            v = v_ref[:, cols]
            if qk_norm:
                k = _head_rms(k, kn_ref[...])
            if write_kv:
                ko_ref[:, cols] = k
                vo_ref[:, cols] = v
            if rope:
                k = _rope(k, ck_ref[...], slk_ref[...], shk_ref[...])
            kb_s[:, cols] = k.astype(BF16)
            vb_s[:, cols] = v.astype(BF16)
        if has_cache:
            kcb_s[...] = kc_ref[...].astype(BF16)
            vcb_s[...] = vc_ref[...].astype(BF16)

    scale = HEAD_DIM ** -0.5
    group = HEADS // n_kv
    if key_window is None:
        krows = slice(None)
    else:
        n_win, starts = key_window
        start = 0
        for blk, st in enumerate(starts):
            start = jnp.where(pl.program_id(1) == blk, st, start)
        krows = pl.ds(pl.multiple_of(start, 256), n_win)
    for kvh in range(n_kv):
        cols = slice(kvh * HEAD_DIM, (kvh + 1) * HEAD_DIM)
        kb = kb_s[krows, cols]
        vb = vb_s[krows, cols]
        if has_cache:
            kcb = kcb_s[:, cols]
            vcb = vcb_s[:, cols]
        for gi in range(group):
            h = kvh * group + gi
            hc = slice(h * HEAD_DIM, (h + 1) * HEAD_DIM)
            q = q_ref[:, hc]
            if qk_norm:
                q = _head_rms(q, qn_ref[...])
            if rope:
                q = _rope(q, cq_ref[...], slq_ref[...], shq_ref[...])
            qb = q.astype(BF16)
            s1 = _dot_nt(qb, kb) * scale
            if has_bias:
                s1 = s1 + bias_ref[h]
            mx = jnp.max(s1, axis=-1, keepdims=True)
            if has_cache:
                s2 = _dot_nt(qb, kcb) * scale
                mx = jnp.maximum(mx, jnp.max(s2, axis=-1, keepdims=True))
            p1 = jnp.exp(s1 - mx)
            den = jnp.sum(p1, axis=-1, keepdims=True)
            acc = _dot(p1.astype(BF16), vb)
            if has_cache:
                p2 = jnp.exp(s2 - mx)
                den = den + jnp.sum(p2, axis=-1, keepdims=True)
                acc = acc + _dot(p2.astype(BF16), vcb)
            o_ref[:, hc] = ((acc / den) * _silu(g_ref[:, hc])).astype(BF16)


def _attention(p_main, batch, seq, *, col_q, col_g, k_spec, v_spec, n_kv, cache=None, bias=None,
               qk_norm=None, rope=None, write_kv=False):
    tq = 256
    nq = seq // tq
    kvw = n_kv * HEAD_DIM
    in_specs = [pl.BlockSpec((tq, GROUP_W), lambda b, i: (b * nq + i, col_q)),
                pl.BlockSpec((seq, kvw), lambda b, i: (b, k_spec)),
                pl.BlockSpec((seq, kvw), lambda b, i: (b, v_spec)),
                pl.BlockSpec((tq, GROUP_W), lambda b, i: (b * nq + i, col_g))]
    args = [p_main, p_main, p_main, p_main]
    if cache is not None:
        kc, vc, layer = cache
        lc = kc.shape[2]
        in_specs += [pl.BlockSpec((None, None, lc, kvw), lambda b, i: (b, layer, 0, 0))] * 2
        args += [kc, vc]
    key_window = None
    if bias is not None:
        table, layer_b, tile_type, key_start = bias
        key_window = (table.shape[-1], key_start)

        def bias_map(b, i):
            t = 0
            for blk, ty in enumerate(tile_type):
                t = jnp.where(i == blk, ty, t)
            return (layer_b, 0, t, 0, 0)

        in_specs.append(pl.BlockSpec((None, HEADS, None, tq, table.shape[-1]), bias_map))
        args.append(table)
    if qk_norm is not None:
        in_specs += [pl.BlockSpec((1, HEAD_DIM), lambda b, i: (0, 0))] * 2
        args += list(qk_norm)
    if rope is not None:
        in_specs += [pl.BlockSpec((tq, HEAD_DIM), lambda b, i: (i, 0))] * 3
        in_specs += [pl.BlockSpec((seq, HEAD_DIM), lambda b, i: (0, 0))] * 3
        args += list(rope) + list(rope)
    out_shape = [jax.ShapeDtypeStruct((batch * seq, GROUP_W), BF16)]
    out_specs = [pl.BlockSpec((tq, GROUP_W), lambda b, i: (b * nq + i, 0))]
    if write_kv:
        assert nq == 1
        out_shape += [jax.ShapeDtypeStruct((batch * seq, kvw), F32)] * 2
        out_specs += [pl.BlockSpec((seq, kvw), lambda b, i: (b, 0))] * 2
    scratch = [pltpu.VMEM((seq, kvw), BF16)] * 2
    if cache is not None:
        scratch += [pltpu.VMEM((lc, kvw), BF16)] * 2
    res = pl.pallas_call(
        functools.partial(_attn_kernel, n_kv=n_kv, has_cache=cache is not None, has_bias=bias is not None,
                          qk_norm=qk_norm is not None, rope=rope is not None, write_kv=write_kv,
                          key_window=key_window),
        out_shape=tuple(out_shape),
        grid=(batch, nq),
        in_specs=in_specs,
        out_specs=tuple(out_specs),
        scratch_shapes=scratch,
        compiler_params=_params("parallel", "arbitrary"),
        name="attention",
    )(*args)
    return res


def _mlstm_kernel(*refs, seq, hp, has_state, write_state):
    it = iter(refs)
    q_ref, k_ref, v_ref, og_ref, g_ref, gates_ref, gb_ref, onw_ref = (next(it) for _ in range(8))
    if has_state:
        c0_ref, n0_ref, m0_ref = next(it), next(it), next(it)
    out_ref = next(it)
    if write_state:
        cout_ref, nout_ref, mout_ref = next(it), next(it), next(it)
    (ht_s,) = (next(it),)

    L = ML_CHUNK
    nc = seq // L
    hd = HEAD_DIM
    scale = hd ** -0.5
    gates = gates_ref[...] + gb_ref[...]
    lane = lax.broadcasted_iota(jnp.int32, gates.shape, 1)
    gates = jnp.where((lane // HEADS) % 2 == 1, _log_sigmoid(gates), gates)
    gt = gates.T[0:4 * HEADS, :]
    gate_id = lax.broadcasted_iota(jnp.int32, gt.shape, 0)

    def head_of(j):
        return j if hp == HEADS else pl.program_id(1) * hp + j

    def gate_row(r):
        if isinstance(r, int):
            return gt[r:r + 1, :]
        return jnp.sum(jnp.where(gate_id == r, gt, 0.0), axis=0, keepdims=True)

    si = lax.broadcasted_iota(jnp.int32, (L, L), 0)
    ti = lax.broadcasted_iota(jnp.int32, (L, L), 1)
    pick3 = jnp.where(lax.broadcasted_iota(jnp.int32, (16, hd), 0) < 3, 1.0, 0.0).astype(BF16)
    row16 = lax.broadcasted_iota(jnp.int32, (16, L), 0)

    def stack3(x):
        x1, x2, x3 = (p.astype(F32) for p in _split3(x))
        stacked = jnp.where(row16 == 0, x1, jnp.where(row16 == 1, x2, jnp.where(row16 == 2, x3, 0.0)))
        return stacked.astype(BF16)

    masks = [si <= ti, si >= ti]
    tris = [jnp.where(mk, 1.0, 0.0).astype(BF16) for mk in masks]
    edges = [L - 1, 0]
    units = [(j, d) for j in range(hp) for d in range(2)]
    ii_row, ff_row, state = {}, {}, {}
    for j, d in units:
        h = head_of(j)
        ii_row[j, d] = gate_row(d * 8 + h)
        ff_row[j, d] = gate_row(d * 8 + HEADS + h)
        if has_state:
            state[j, d] = (c0_ref[d, j].T, n0_ref[d, pl.ds(h, 1), :], m0_ref[d, pl.ds(h, 1), 0:1])
        else:
            state[j, d] = (jnp.zeros((hd, hd), F32), jnp.zeros((1, hd), F32), jnp.zeros((1, 1), F32))

    for step in range(nc):
        chunk = [step, nc - 1 - step]
        rows = {u: slice(chunk[u[1]] * L, (chunk[u[1]] + 1) * L) for u in units}
        b, g, g_col, g_src, mx = {}, {}, {}, {}, {}
        for u in units:
            b3 = _dot(stack3(ff_row[u][:, rows[u]]), tris[u[1]])
            b[u] = b3[0:1, :] + b3[1:2, :] + b3[2:3, :]
            g[u] = ii_row[u][:, rows[u]] - b[u]
        for u in units:
            g_col[u] = _dot_tn(stack3(g[u]), pick3)
        for u in units:
            g_src[u] = jnp.where(masks[u[1]], jnp.concatenate([g_col[u]] * (L // hd), axis=1), NEG_INF)
            mx[u] = jnp.maximum(state[u][2], jnp.max(g_src[u], axis=0, keepdims=True))
        for u in units:
            j, d = u
            hc = slice(j * hd, (j + 1) * hd)
            ct, n, m = state[u]
            qb = (q_ref[rows[u], hc] * scale).astype(BF16)
            k = k_ref[rows[u], hc]
            kb, vb = k.astype(BF16), v_ref[rows[u], hc].astype(BF16)
            p = _dot_nt(kb, qb) * jnp.exp(g_src[u] - mx[u])
            w_inter = jnp.exp(m - mx[u])
            n16 = jnp.broadcast_to(n, (16, hd)).astype(BF16)
            den = jnp.sum(p, axis=0, keepdims=True) + w_inter * _dot_nt(n16, qb)[0:1, :]
            num = _dot_tn(vb, p.astype(BF16)) + w_inter * _dot_nt(ct.astype(BF16), qb)
            ht_s[d, j, :, rows[u]] = num / jnp.maximum(jnp.abs(den), jnp.exp(-(b[u] + mx[u])))
            mx_end = mx[u][:, edges[d]:edges[d] + 1]
            w_prev = jnp.exp(m - mx_end)
            ku = k * jnp.exp(g_col[u] - mx_end)
            state[u] = (w_prev * ct + _dot_tn(vb, ku.astype(BF16)),
                        w_prev * n + jnp.sum(ku, axis=0, keepdims=True),
                        b[u][:, edges[d]:edges[d] + 1] + mx_end)

    if write_state:
        for j, d in units:
            ct, n, m = state[j, d]
            cout_ref[d, j] = ct.T
            nout_ref[d, pl.ds(head_of(j), 1), :] = n
            mout_ref[d, pl.ds(head_of(j), 1), :] = jnp.broadcast_to(m, (1, hd))

    for j in range(hp):
        hc = slice(j * hd, (j + 1) * hd)
        hs = (ht_s[0, j] + ht_s[1, j]).T
        y = hs * lax.rsqrt(jnp.mean(hs * hs, axis=-1, keepdims=True) + NORM_EPS) * onw_ref[:, hc]
        out_ref[:, hc] = (y * _sigmoid(og_ref[:, hc]) * _silu(g_ref[:, hc])).astype(BF16)


def _mlstm(p_main, p_if, batch, seq, gate_b, out_norm_w, hp, state=None, write_state=False):
    hd = HEAD_DIM
    nh = HEADS // hp
    w = hp * hd
    col = lambda base: (lambda b, h: (b, base * nh + h))
    in_specs = [pl.BlockSpec((seq, w), col(COL_ML_Q)), pl.BlockSpec((seq, w), col(COL_ML_K)),
                pl.BlockSpec((seq, w), col(COL_ML_V)), pl.BlockSpec((seq, w), col(COL_ML_O)),
                pl.BlockSpec((seq, w), col(COL_ML_G)),
                pl.BlockSpec((seq, 128), lambda b, h: (b, 0)),
                pl.BlockSpec((1, 128), lambda b, h: (0, 0)),
                pl.BlockSpec((1, w), lambda b, h: (0, h))]
    args = [p_main] * 5 + [p_if, jnp.pad(gate_b, (0, 128 - gate_b.shape[0])).reshape(1, 128),
                           out_norm_w.reshape(1, GROUP_W)]
    if state is not None:
        c0, n0, m0, layer = state
        m0b = jnp.broadcast_to(m0[..., None], m0.shape + (hd,))
        in_specs += [pl.BlockSpec((None, None, 2, hp, hd, hd), lambda b, h: (b, layer, 0, h, 0, 0)),
                     pl.BlockSpec((None, None, 2, HEADS, hd), lambda b, h: (b, layer, 0, 0, 0)),
                     pl.BlockSpec((None, None, 2, HEADS, hd), lambda b, h: (b, layer, 0, 0, 0))]
        args += [c0, n0, m0b]
    out_shape = [jax.ShapeDtypeStruct((batch * seq, GROUP_W), BF16)]
    out_specs = [pl.BlockSpec((seq, w), lambda b, h: (b, h))]
    if write_state:
        out_shape += [jax.ShapeDtypeStruct((batch, 2, HEADS, hd, hd), F32),
                      jax.ShapeDtypeStruct((batch, 2, HEADS, hd), F32),
                      jax.ShapeDtypeStruct((batch, 2, HEADS, hd), F32)]
        out_specs += [pl.BlockSpec((None, 2, hp, hd, hd), lambda b, h: (b, 0, h, 0, 0)),
                      pl.BlockSpec((None, 2, HEADS, hd), lambda b, h: (b, 0, 0, 0)),
                      pl.BlockSpec((None, 2, HEADS, hd), lambda b, h: (b, 0, 0, 0))]
    return pl.pallas_call(
        functools.partial(_mlstm_kernel, seq=seq, hp=hp, has_state=state is not None, write_state=write_state),
        out_shape=tuple(out_shape),
        grid=(batch, nh),
        in_specs=in_specs,
        out_specs=tuple(out_specs),
        scratch_shapes=[pltpu.VMEM((2, hp, hd, seq), F32)],
        compiler_params=_params("parallel", "arbitrary"),
        name="mlstm",
    )(*args)


NA_Q_ROWS = 4
NA_K_ROWS = 12


def _na_bias(rpb, seq):
    rows = seq // GRID_W
    wr = min(NA_ROWS, rows)
    assert rows % NA_Q_ROWS == 0 and rows >= NA_K_ROWS
    c = np.arange(GRID_W)
    c_start = np.clip(c - NA_COLS // 2, 0, GRID_W - NA_COLS)
    col_ok = (c[None, :] >= c_start[:, None]) & (c[None, :] < c_start[:, None] + NA_COLS)
    col_off = np.clip(c[None, :] - c[:, None] + NA_COLS - 1, 0, 2 * NA_COLS - 2)
    col_sel = (col_off[..., None] == np.arange(2 * NA_COLS - 1)).astype(np.float32)
    blocks = jnp.einsum('lhij,cdj->lhicd', rpb.astype(F32), col_sel, precision=lax.Precision.HIGHEST)
    blocks = jnp.where(col_ok, blocks, NEG_INF)
    blocks = jnp.concatenate([blocks, jnp.full_like(blocks[:, :, :1], NEG_INF)], axis=2)
    masked = 2 * NA_ROWS - 1
    nblk = rows // NA_Q_ROWS
    key_row0 = [int(min(np.clip(i * NA_Q_ROWS - wr // 2, 0, rows - wr), rows - NA_K_ROWS)) for i in range(nblk)]
    idx = np.full((nblk, NA_Q_ROWS, NA_K_ROWS), masked, np.int32)
    for i in range(nblk):
        for a in range(NA_Q_ROWS):
            qr = i * NA_Q_ROWS + a
            r_start = int(np.clip(qr - wr // 2, 0, rows - wr))
            assert key_row0[i] <= r_start and r_start + wr <= key_row0[i] + NA_K_ROWS
            for kk in range(NA_K_ROWS):
                kr = key_row0[i] + kk
                if r_start <= kr < r_start + wr:
                    idx[i, a, kk] = kr - qr + NA_ROWS - 1
    kinds, kind_of = np.unique(idx.reshape(nblk, -1), axis=0, return_inverse=True)
    kinds = kinds.reshape(-1, NA_Q_ROWS, NA_K_ROWS)
    table = blocks[:, :, kinds]
    table = jnp.transpose(table, (0, 1, 2, 3, 5, 4, 6))
    table = table.reshape(rpb.shape[0], rpb.shape[1], kinds.shape[0], NA_Q_ROWS * GRID_W, NA_K_ROWS * GRID_W)
    return table, [int(t) for t in np.ravel(kind_of)], [r * GRID_W for r in key_row0]


def _rope_tables(seq):
    t = jnp.arange(seq)
    nf = HEAD_DIM // 4
    inv = ROPE_THETA ** (-jnp.arange(nf, dtype=F32) / nf)
    zeros = jnp.zeros((seq, nf), F32)

    def half(pos):
        ang = pos.astype(F32)[:, None] * inv
        cs, sn = jnp.cos(ang), jnp.sin(ang)
        return jnp.concatenate([cs, cs], -1), jnp.concatenate([-sn, zeros], -1), jnp.concatenate([zeros, sn], -1)

    parts = [half(t // GRID_W), half(t % GRID_W)]
    return tuple(jnp.concatenate([parts[0][i], parts[1][i]], -1) for i in range(3))


def kernel(x_prompt, x_sample, cache_na_k, cache_na_v, cache_gqa_k, cache_gqa_v, state_lru, state_mlstm_C,
           state_mlstm_n, state_mlstm_m, c, c_ctx, norm_w, ada_w, ada_b, w_in, lru_conv_w, lru_conv_b, lru_wr,
           lru_br, lru_wi, lru_bi, lru_lambda, na_rpb, gqa_qnorm, gqa_knorm, ml_gate_b, ml_out_norm, w_out,
           final_norm_w):
    bp, tp, d = x_prompt.shape
    bs, ts, _ = x_sample.shape
    depth = w_in.shape[0]
    past = cache_na_k.shape[2]

    cond = jnp.zeros((COND_ROWS, d), F32).at[0].set(c_ctx).at[1:1 + bs].set(c)
    mod4 = _modulation(cond, ada_w, ada_b).reshape(depth, COND_ROWS, 1, 3 * d)

    w_main = w_in[:, :, :MAIN_W].astype(BF16)
    w_if = jnp.pad(w_in[:, :, MAIN_W:], ((0, 0), (0, 0), (0, 128 - (w_in.shape[2] - MAIN_W)))).astype(BF16)
    w_out_b = w_out.astype(BF16)
    w_gates = jnp.concatenate([lru_wr[:, 0], lru_wi[:, 0], lru_wr[:, 1], lru_wi[:, 1]], axis=-1).astype(BF16)
    bias_table, bias_kind, bias_key0 = _na_bias(na_rpb, ts)
    rope = _rope_tables(ts)
    ck_na = cache_na_k.reshape(bs, depth, past, GROUP_W)
    cv_na = cache_na_v.reshape(bs, depth, past, GROUP_W)
    kvw = GQA_KV_HEADS * HEAD_DIM
    ck_gqa = cache_gqa_k.reshape(bs, depth, past, kvw)
    cv_gqa = cache_gqa_v.reshape(bs, depth, past, kvw)
    zeros_h0 = jnp.zeros((bp, 2, GROUP_W), F32)

    xp = x_prompt.reshape(bp * tp, d)
    xs = x_sample.reshape(bs * ts, d)
    norm_w2 = norm_w.reshape(depth, 1, d)
    fnw = final_norm_w.reshape(1, d)
    new = {k: [] for k in ('na_k', 'na_v', 'gqa_k', 'gqa_v', 'lru', 'C', 'n', 'm')}

    for l in range(depth):
        last = l == depth - 1
        lru_args = (lru_conv_w[l], lru_conv_b[l].reshape(1, GROUP_W), w_gates[l], lru_br[l], lru_bi[l], lru_lambda[l])
        qk_norm = (gqa_qnorm[l].reshape(1, HEAD_DIM), gqa_knorm[l].reshape(1, HEAD_DIM))

        p, pif = _in_projection(xp, norm_w2[l], mod4, l, 0, bp * tp, w_main[l], w_if[l])
        out_a, st_lru = _lru(p, bp, tp, *lru_args, zeros_h0)
        out_b, na_k, na_v = _attention(p, bp, tp, col_q=COL_NA_Q, col_g=COL_NA_G, k_spec=COL_NA_K,
                                       v_spec=COL_NA_V, n_kv=HEADS, write_kv=True)
        out_c, gqa_k, gqa_v = _attention(p, bp, tp, col_q=COL_GQA_Q, col_g=COL_GQA_G, k_spec=2 * COL_GQA_KV,
                                         v_spec=2 * COL_GQA_KV + 1, n_kv=GQA_KV_HEADS, qk_norm=qk_norm,
                                         write_kv=True)
        out_d, st_c, st_n, st_m = _mlstm(p, pif, bp, tp, ml_gate_b[l], ml_out_norm[l], HEADS, write_state=True)
        xp = _out_projection((out_a, out_b, out_c, out_d), w_out_b[l], xp, mod4, l, 0, bp * tp,
                             fnw if last else None)
        new['na_k'].append(na_k.reshape(bp, tp, HEADS, HEAD_DIM))
        new['na_v'].append(na_v.reshape(bp, tp, HEADS, HEAD_DIM))
        new['gqa_k'].append(gqa_k.reshape(bp, tp, GQA_KV_HEADS, HEAD_DIM))
        new['gqa_v'].append(gqa_v.reshape(bp, tp, GQA_KV_HEADS, HEAD_DIM))
        new['lru'].append(st_lru)
        new['C'].append(st_c)
        new['n'].append(st_n)
        new['m'].append(st_m[..., 0])

        p, pif = _in_projection(xs, norm_w2[l], mod4, l, 1, ts, w_main[l], w_if[l])
        out_a, _ = _lru(p, bs, ts, *lru_args, state_lru[:, l])
        (out_b,) = _attention(p, bs, ts, col_q=COL_NA_Q, col_g=COL_NA_G, k_spec=COL_NA_K, v_spec=COL_NA_V,
                              n_kv=HEADS, cache=(ck_na, cv_na, l), bias=(bias_table, l, bias_kind, bias_key0))
        (out_c,) = _attention(p, bs, ts, col_q=COL_GQA_Q, col_g=COL_GQA_G, k_spec=2 * COL_GQA_KV,
                              v_spec=2 * COL_GQA_KV + 1, n_kv=GQA_KV_HEADS, cache=(ck_gqa, cv_gqa, l),
                              qk_norm=qk_norm, rope=rope)
        (out_d,) = _mlstm(p, pif, bs, ts, ml_gate_b[l], ml_out_norm[l], 2,
                          state=(state_mlstm_C, state_mlstm_n, state_mlstm_m, l))
        xs = _out_projection((out_a, out_b, out_c, out_d), w_out_b[l], xs, mod4, l, 1, ts,
                             fnw if last else None)

    stacked = {k: jnp.stack(v, axis=1) for k, v in new.items()}
    return (xp.reshape(bp, tp, d), xs.reshape(bs, ts, d), stacked['na_k'], stacked['na_v'], stacked['gqa_k'],
            stacked['gqa_v'], stacked['lru'], stacked['C'], stacked['n'], stacked['m'])
```

```python
import functools
from typing import NamedTuple

import numpy as np
import jax
import jax.numpy as jnp
from jax import lax
from jax.experimental import pallas as pl
from jax.experimental.pallas import tpu as pltpu

F32 = jnp.float32
BF16 = jnp.bfloat16

HEAD_DIM = 128
GROUP_W = 512
HEADS = GROUP_W // HEAD_DIM
GQA_KV_HEADS = 2
GRID_W = 64
NORM_EPS = 1e-6
NEG_INF = -1e30
LRU_C = 8.0
NA_ROWS, NA_COLS = 8, 16
ROPE_THETA = 10000.0
ML_CHUNK = 256
COND_ROWS = 16
ROW_CHUNK = 256
VMEM_LIMIT = 52 * 1024 * 1024

COL_LRU_X, COL_LRU_G, COL_NA_Q, COL_NA_K, COL_NA_V, COL_NA_G = 0, 1, 2, 3, 4, 5
COL_GQA_Q, COL_GQA_KV, COL_GQA_G = 6, 7, 8
COL_ML_Q, COL_ML_K, COL_ML_V, COL_ML_O, COL_ML_G = 9, 10, 11, 12, 13
MAIN_W = 14 * GROUP_W


def _params(*sem):
    return pltpu.CompilerParams(dimension_semantics=sem, vmem_limit_bytes=VMEM_LIMIT)


def _sigmoid(x):
    return 1.0 / (1.0 + jnp.exp(-x))


def _silu(x):
    return x * _sigmoid(x)


def _log_sigmoid(x):
    return jnp.minimum(x, 0.0) - jnp.log1p(jnp.exp(-jnp.abs(x)))


def _dot(a, b):
    return jnp.dot(a, b, preferred_element_type=F32)


def _dot_nt(a, b):
    return lax.dot_general(a, b, (((1,), (1,)), ((), ())), preferred_element_type=F32)


def _dot_tn(a, b):
    return lax.dot_general(a, b, (((0,), (0,)), ((), ())), preferred_element_type=F32)


def _split3(x):
    x1 = x.astype(BF16)
    r = x - x1.astype(F32)
    x2 = r.astype(BF16)
    x3 = (r - x2.astype(F32)).astype(BF16)
    return x1, x2, x3


def _mod_kernel(cond_ref, w_ref, b_ref, o_ref):
    s = _silu(cond_ref[...]).astype(BF16)
    o_ref[...] = _dot(s, w_ref[...].astype(BF16)) + b_ref[...]


def _modulation(cond, ada_w, ada_b):
    depth, d, n = ada_w.shape
    tn = 1024
    return pl.pallas_call(
        _mod_kernel,
        out_shape=jax.ShapeDtypeStruct((depth, COND_ROWS, n), F32),
        grid=(depth, n // tn),
        in_specs=[pl.BlockSpec((COND_ROWS, d), lambda l, j: (0, 0)),
                  pl.BlockSpec((None, d, tn), lambda l, j: (l, 0, j)),
                  pl.BlockSpec((None, 1, tn), lambda l, j: (l, 0, j))],
        out_specs=pl.BlockSpec((None, COND_ROWS, tn), lambda l, j: (l, 0, j)),
        compiler_params=_params("parallel", "parallel"),
        name="adaln_mod",
    )(cond, ada_w, ada_b.reshape(depth, 1, n))


def _inproj_kernel(x_ref, nw_ref, sh_ref, sc_ref, w_ref, wif_ref, o_ref, oif_ref, xm_ref):
    @pl.when(pl.program_id(1) == 0)
    def _():
        for r in range(x_ref.shape[0] // ROW_CHUNK):
            rows = slice(r * ROW_CHUNK, (r + 1) * ROW_CHUNK)
            x = x_ref[rows, :]
            y = x * lax.rsqrt(jnp.mean(x * x, axis=-1, keepdims=True) + NORM_EPS) * nw_ref[...]
            xm_ref[rows, :] = (y * (1.0 + sc_ref[...]) + sh_ref[...]).astype(BF16)
        oif_ref[...] = _dot(xm_ref[...], wif_ref[...])

    o_ref[...] = _dot(xm_ref[...], w_ref[...]).astype(o_ref.dtype)


def _in_projection(x, norm_w, mod4, layer, mod_row0, rows_per_mod, w_main, w_if):
    m, d = x.shape
    tm, tn = min(1024, m), 1024
    assert m % tm == 0 and rows_per_mod % tm == 0
    per_batch = rows_per_mod < m

    def mod_map(col):
        if per_batch:
            return lambda i, j: (layer, mod_row0 + (i * tm) // rows_per_mod, 0, col)
        return lambda i, j: (layer, mod_row0, 0, col)

    return pl.pallas_call(
        _inproj_kernel,
        out_shape=(jax.ShapeDtypeStruct((m, MAIN_W), BF16), jax.ShapeDtypeStruct((m, 128), F32)),
        grid=(m // tm, MAIN_W // tn),
        in_specs=[pl.BlockSpec((tm, d), lambda i, j: (i, 0)),
                  pl.BlockSpec((1, d), lambda i, j: (0, 0)),
                  pl.BlockSpec((None, None, 1, d), mod_map(0)),
                  pl.BlockSpec((None, None, 1, d), mod_map(1)),
                  pl.BlockSpec((d, tn), lambda i, j: (0, j)),
                  pl.BlockSpec((d, 128), lambda i, j: (0, 0))],
        out_specs=(pl.BlockSpec((tm, tn), lambda i, j: (i, j)),
                   pl.BlockSpec((tm, 128), lambda i, j: (i, 0))),
        scratch_shapes=[pltpu.VMEM((tm, d), BF16)],
        compiler_params=_params("parallel", "arbitrary"),
        name="in_proj",
    )(x, norm_w, mod4, mod4, w_main, w_if)


def _outproj_kernel(a_ref, b_ref, c_ref, d_ref, w_ref, x_ref, g_ref, *rest, final):
    o_ref = rest[-1]
    for n in range(w_ref.shape[1] // GROUP_W):
        cols = slice(n * GROUP_W, (n + 1) * GROUP_W)
        acc = _dot(a_ref[...], w_ref[0:GROUP_W, cols])
        acc += _dot(b_ref[...], w_ref[GROUP_W:2 * GROUP_W, cols])
        acc += _dot(c_ref[...], w_ref[2 * GROUP_W:3 * GROUP_W, cols])
        acc += _dot(d_ref[...], w_ref[3 * GROUP_W:4 * GROUP_W, cols])
        o_ref[:, cols] = x_ref[:, cols] + g_ref[:, cols] * acc
    if final:
        fnw_ref = rest[0]
        for r in range(o_ref.shape[0] // ROW_CHUNK):
            rows = slice(r * ROW_CHUNK, (r + 1) * ROW_CHUNK)
            xn = o_ref[rows, :]
            o_ref[rows, :] = xn * lax.rsqrt(jnp.mean(xn * xn, axis=-1, keepdims=True) + NORM_EPS) * fnw_ref[...]


def _out_projection(branches, w_out, x, mod4, layer, mod_row0, rows_per_mod, final_norm_w):
    m, d = x.shape
    tm = 512
    per_batch = rows_per_mod < m
    if per_batch:
        gate_map = lambda i: (layer, mod_row0 + (i * tm) // rows_per_mod, 0, 2)
    else:
        gate_map = lambda i: (layer, mod_row0, 0, 2)
    final = final_norm_w is not None
    in_specs = [pl.BlockSpec((tm, GROUP_W), lambda i: (i, 0)) for _ in range(4)]
    in_specs += [pl.BlockSpec((d, d), lambda i: (0, 0)),
                 pl.BlockSpec((tm, d), lambda i: (i, 0)),
                 pl.BlockSpec((None, None, 1, d), gate_map)]
    args = list(branches) + [w_out, x, mod4]
    if final:
        in_specs.append(pl.BlockSpec((1, d), lambda i: (0, 0)))
        args.append(final_norm_w)
    return pl.pallas_call(
        functools.partial(_outproj_kernel, final=final),
        out_shape=jax.ShapeDtypeStruct((m, d), F32),
        grid=(m // tm,),
        in_specs=in_specs,
        out_specs=pl.BlockSpec((tm, d), lambda i: (i, 0)),
        compiler_params=_params("parallel"),
        name="out_proj",
    )(*args)


def _scan8(a, b, row, reverse):
    for dist in (1, 2, 4):
        if reverse:
            keep = row < 8 - dist
            shift = 8 - dist
        else:
            keep = row >= dist
            shift = dist
        a_sh = jnp.where(keep, pltpu.roll(a, shift, 0), 1.0)
        b_sh = jnp.where(keep, pltpu.roll(b, shift, 0), 0.0)
        b = b + a * b_sh
        a = a * a_sh
    return a, b


def _lru_kernel(x_ref, g_ref, cw_ref, cb_ref, wg_ref, br_ref, bi_ref, lam_ref, h0_ref,
                o_ref, st_ref, xc_s, pre_s, a_s, b_s, h_s, *, seq):
    w = GROUP_W
    rc = 256
    x = x_ref[...].astype(F32)
    t = lax.broadcasted_iota(jnp.int32, (seq, w), 0)
    xc = x * cw_ref[1:2, :] + cb_ref[...]
    xc += jnp.where(t >= 1, pltpu.roll(x, 1, 0), 0.0) * cw_ref[0:1, :]
    xc += jnp.where(t < seq - 1, pltpu.roll(x, seq - 1, 0), 0.0) * cw_ref[2:3, :]
    xc += jnp.where(t < seq - 2, pltpu.roll(x, seq - 2, 0), 0.0) * cw_ref[3:4, :]
    xc_s[...] = xc
    for g in range(HEADS):
        pre = _dot(xc_s[:, g * 128:(g + 1) * 128].astype(BF16), wg_ref[g])
        for k in range(4):
            pre_s[k, :, g * 128:(g + 1) * 128] = pre[:, k * 128:(k + 1) * 128]
    for d in range(2):
        log_lam = LRU_C * _log_sigmoid(lam_ref[d:d + 1, :])
        for c in range(seq // rc):
            rows = slice(c * rc, (c + 1) * rc)
            r = _sigmoid(pre_s[2 * d, rows, :] + br_ref[d:d + 1, :])
            i = _sigmoid(pre_s[2 * d + 1, rows, :] + bi_ref[d:d + 1, :])
            log_a = log_lam * r
            a = jnp.exp(log_a)
            a_s[d, rows, :] = a
            b_s[d, rows, :] = jnp.sqrt(-jnp.tanh(log_a) * (a * a + 1.0)) * i * xc_s[rows, :]

    row = lax.broadcasted_iota(jnp.int32, (8, w), 0)
    ngroups = seq // 8

    def body(g, carry):
        hf, hb = carry
        rf = pl.multiple_of(g * 8, 8)
        rb = pl.multiple_of((ngroups - 1 - g) * 8, 8)
        af, bf = _scan8(a_s[0, pl.ds(rf, 8), :], b_s[0, pl.ds(rf, 8), :], row, False)
        ab, bb = _scan8(a_s[1, pl.ds(rb, 8), :], b_s[1, pl.ds(rb, 8), :], row, True)
        hf_new = bf + af * hf
        hb_new = bb + ab * hb
        h_s[0, pl.ds(rf, 8), :] = hf_new
        h_s[1, pl.ds(rb, 8), :] = hb_new
        return (jnp.broadcast_to(hf_new[7:8, :], (8, w)), jnp.broadcast_to(hb_new[0:1, :], (8, w)))

    hf0 = jnp.broadcast_to(h0_ref[0:1, :], (8, w))
    hb0 = jnp.broadcast_to(h0_ref[1:2, :], (8, w))
    hf_last, hb_last = lax.fori_loop(0, ngroups, body, (hf0, hb0))
    st_ref[0:1, :] = hf_last[0:1, :]
    st_ref[1:2, :] = hb_last[0:1, :]
    for c in range(seq // rc):
        rows = slice(c * rc, (c + 1) * rc)
        o_ref[rows, :] = ((h_s[0, rows, :] + h_s[1, rows, :]) * _silu(g_ref[rows, :].astype(F32))).astype(BF16)


def _lru(p_main, batch, seq, conv_w, conv_b, w_gates, b_r, b_i, lam, h0):
    w = GROUP_W
    const = lambda *shape: pl.BlockSpec(shape, lambda b: (0,) * len(shape))
    return pl.pallas_call(
        functools.partial(_lru_kernel, seq=seq),
        out_shape=(jax.ShapeDtypeStruct((batch * seq, w), BF16), jax.ShapeDtypeStruct((batch, 2, w), F32)),
        grid=(batch,),
        in_specs=[pl.BlockSpec((seq, w), lambda b: (b, COL_LRU_X)),
                  pl.BlockSpec((seq, w), lambda b: (b, COL_LRU_G)),
                  const(4, w), const(1, w), const(HEADS, 128, 4 * 128), const(2, w), const(2, w), const(2, w),
                  pl.BlockSpec((None, 2, w), lambda b: (b, 0, 0))],
        out_specs=(pl.BlockSpec((seq, w), lambda b: (b, 0)),
                   pl.BlockSpec((None, 2, w), lambda b: (b, 0, 0))),
        scratch_shapes=[pltpu.VMEM((seq, w), F32), pltpu.VMEM((4, seq, w), F32),
                        pltpu.VMEM((2, seq, w), F32), pltpu.VMEM((2, seq, w), F32), pltpu.VMEM((2, seq, w), F32)],
        compiler_params=_params("parallel"),
        name="rg_lru",
    )(p_main, p_main, conv_w, conv_b, w_gates, b_r, b_i, lam, h0)


def _head_rms(x, w):
    return x * lax.rsqrt(jnp.mean(x * x, axis=-1, keepdims=True) + NORM_EPS) * w


def _rope(x, cos, sin_lo, sin_hi):
    return x * cos + pltpu.roll(x, 96, 1) * sin_lo + pltpu.roll(x, 32, 1) * sin_hi


def _attn_kernel(*refs, n_kv, has_cache, bias_plan, qk_norm, rope, write_kv):
    has_bias = bias_plan is not None
    it = iter(refs)
    q_ref, k_ref, v_ref, g_ref = next(it), next(it), next(it), next(it)
    kc_ref = vc_ref = bias_ref = qn_ref = kn_ref = None
    if has_cache:
        kc_ref, vc_ref = next(it), next(it)
    if has_bias:
        bias_ref = next(it)
    if qk_norm:
        qn_ref, kn_ref = next(it), next(it)
    if rope:
        cq_ref, slq_ref, shq_ref, ck_ref, slk_ref, shk_ref = (next(it) for _ in range(6))
    if write_kv:
        next(it), next(it)
    o_ref = next(it)
    if write_kv:
        ko_ref, vo_ref = next(it), next(it)
    kb_s, vb_s = next(it), next(it)
    if has_cache:
        kcb_s, vcb_s = next(it), next(it)
    if has_bias:
        bias_s = next(it)
        @pl.when(jnp.logical_and(pl.program_id(0) == 0, pl.program_id(1) == 0))
        def _():
            for t, kind in enumerate(bias_plan.kinds):
                for h in range(HEADS):
                    for a, row in enumerate(kind):
                        for p, entry in enumerate(row):
                            bias_s[t, h, a * GRID_W:(a + 1) * GRID_W, p * 128:(p + 1) * 128] = bias_ref[h, entry]

    @pl.when(pl.program_id(1) == 0)
    def _():
        for kvh in range(n_kv):
            cols = slice(kvh * HEAD_DIM, (kvh + 1) * HEAD_DIM)
            k = k_ref[:, cols]
            v = v_ref[:, cols]
            if qk_norm:
                k = _head_rms(k.astype(F32), kn_ref[...])
            if write_kv:
                ko_ref[:, kvh, :] = k.astype(F32)
                vo_ref[:, kvh, :] = v.astype(F32)
            if rope:
                k = _rope(k.astype(F32), ck_ref[...], slk_ref[...], shk_ref[...])
            kb_s[:, cols] = k.astype(BF16)
            vb_s[:, cols] = v.astype(BF16)
            if has_cache:
                kcb_s[:, cols] = kc_ref[:, kvh, :].astype(BF16)
                vcb_s[:, cols] = vc_ref[:, kvh, :].astype(BF16)

    scale = HEAD_DIM ** -0.5
    group = HEADS // n_kv
    krows = slice(None)
    if has_bias:
        start, kind_id = 0, 0
        for blk in range(len(bias_plan.kind_of)):
            start = jnp.where(pl.program_id(1) == blk, bias_plan.key_start[blk], start)
            kind_id = jnp.where(pl.program_id(1) == blk, bias_plan.kind_of[blk], kind_id)
        krows = pl.ds(pl.multiple_of(start, 256), bias_plan.window)
    for kvh in range(n_kv):
        cols = slice(kvh * HEAD_DIM, (kvh + 1) * HEAD_DIM)
        kb = kb_s[krows, cols]
        vb = vb_s[krows, cols]
        if has_cache:
            kcb = kcb_s[:, cols]
            vcb = vcb_s[:, cols]
        for gi in range(group):
            h = kvh * group + gi
            hc = slice(h * HEAD_DIM, (h + 1) * HEAD_DIM)
            q = q_ref[:, hc]
            if qk_norm:
                q = _head_rms(q.astype(F32), qn_ref[...])
            if rope:
                q = _rope(q.astype(F32), cq_ref[...], slq_ref[...], shq_ref[...])
            qb = q.astype(BF16)
            s1 = _dot_nt(qb, kb) * scale
            if has_bias:
                s1 = s1 + bias_s[kind_id, h]
            mx = jnp.max(s1, axis=-1, keepdims=True)
            if has_cache:
                s2 = _dot_nt(qb, kcb) * scale
                mx = jnp.maximum(mx, jnp.max(s2, axis=-1, keepdims=True))
            p1 = jnp.exp(s1 - mx)
            den = jnp.sum(p1, axis=-1, keepdims=True)
            acc = _dot(p1.astype(BF16), vb)
            if has_cache:
                p2 = jnp.exp(s2 - mx)
                den = den + jnp.sum(p2, axis=-1, keepdims=True)
                acc = acc + _dot(p2.astype(BF16), vcb)
            o_ref[:, hc] = ((acc / den) * _silu(g_ref[:, hc].astype(F32))).astype(BF16)


def _attention(p_main, batch, seq, *, col_q, col_g, k_spec, v_spec, n_kv, cache=None, bias=None,
               qk_norm=None, rope=None, kv_out=None):
    tq = 256
    nq = seq // tq
    kvw = n_kv * HEAD_DIM
    in_specs = [pl.BlockSpec((tq, GROUP_W), lambda b, i: (b * nq + i, col_q)),
                pl.BlockSpec((seq, kvw), lambda b, i: (b, k_spec)),
                pl.BlockSpec((seq, kvw), lambda b, i: (b, v_spec)),
                pl.BlockSpec((tq, GROUP_W), lambda b, i: (b * nq + i, col_g))]
    args = [p_main, p_main, p_main, p_main]
    scratch = [pltpu.VMEM((seq, kvw), BF16)] * 2
    if cache is not None:
        kc, vc, layer = cache
        lc = kc.shape[2]
        in_specs += [pl.BlockSpec((None, None, lc, n_kv, HEAD_DIM), lambda b, i: (b, layer, 0, 0, 0))] * 2
        args += [kc, vc]
        scratch += [pltpu.VMEM((lc, kvw), BF16)] * 2
    bias_plan = None
    if bias is not None:
        pairs, layer_b, bias_plan = bias
        in_specs.append(pl.BlockSpec((None,) + pairs.shape[1:], lambda b, i: (layer_b, 0, 0, 0, 0)))
        args.append(pairs)
        scratch.append(pltpu.VMEM((len(bias_plan.kinds), HEADS, tq, bias_plan.window), F32))
    if qk_norm is not None:
        in_specs += [pl.BlockSpec((1, HEAD_DIM), lambda b, i: (0, 0))] * 2
        args += list(qk_norm)
    if rope is not None:
        in_specs += [pl.BlockSpec((tq, HEAD_DIM), lambda b, i: (i, 0))] * 3
        in_specs += [pl.BlockSpec((seq, HEAD_DIM), lambda b, i: (0, 0))] * 3
        args += list(rope) + list(rope)
    out_shape = [jax.ShapeDtypeStruct((batch * seq, GROUP_W), BF16)]
    out_specs = [pl.BlockSpec((tq, GROUP_W), lambda b, i: (b * nq + i, 0))]
    aliases = {}
    if kv_out is not None:
        assert nq == 1
        k_all, v_all, layer_o = kv_out
        for j, arr in enumerate((k_all, v_all)):
            aliases[len(args)] = 1 + j
            in_specs.append(pl.BlockSpec(memory_space=pl.ANY))
            args.append(arr)
            out_shape.append(jax.ShapeDtypeStruct(arr.shape, arr.dtype))
            out_specs.append(pl.BlockSpec((None, None, seq, n_kv, HEAD_DIM), lambda b, i: (b, layer_o, 0, 0, 0)))
    return pl.pallas_call(
        functools.partial(_attn_kernel, n_kv=n_kv, has_cache=cache is not None, bias_plan=bias_plan,
                          qk_norm=qk_norm is not None, rope=rope is not None, write_kv=kv_out is not None),
        out_shape=tuple(out_shape),
        grid=(batch, nq),
        in_specs=in_specs,
        out_specs=tuple(out_specs),
        scratch_shapes=scratch,
        input_output_aliases=aliases,
        compiler_params=_params("arbitrary", "arbitrary"),
        name="attention",
    )(*args)


def _mlstm_kernel(*refs, seq, hp, has_state, write_state):
    it = iter(refs)
    q_ref, k_ref, v_ref, og_ref, g_ref, gates_ref, gb_ref, onw_ref = (next(it) for _ in range(8))
    if has_state:
        c0_ref, n0_ref, m0_ref = next(it), next(it), next(it)
    if write_state:
        next(it)
    out_ref = next(it)
    if write_state:
        cout_ref, nout_ref, mout_ref = next(it), next(it), next(it)
    ht_s = next(it)

    L = ML_CHUNK
    nc = seq // L
    hd = HEAD_DIM
    scale = hd ** -0.5
    gates = gates_ref[...] + gb_ref[...]
    lane = lax.broadcasted_iota(jnp.int32, gates.shape, 1)
    gates = jnp.where((lane // HEADS) % 2 == 1, _log_sigmoid(gates), gates)
    gt = gates.T[0:4 * HEADS, :]
    gate_id = lax.broadcasted_iota(jnp.int32, gt.shape, 0)

    def head_of(j):
        return j if hp == HEADS else pl.program_id(1) * hp + j

    def gate_row(r):
        if isinstance(r, int):
            return gt[r:r + 1, :]
        return jnp.sum(jnp.where(gate_id == r, gt, 0.0), axis=0, keepdims=True)

    si = lax.broadcasted_iota(jnp.int32, (L, L), 0)
    ti = lax.broadcasted_iota(jnp.int32, (L, L), 1)
    pick3 = jnp.where(lax.broadcasted_iota(jnp.int32, (16, hd), 0) < 3, 1.0, 0.0).astype(BF16)
    row16 = lax.broadcasted_iota(jnp.int32, (16, L), 0)

    def stack3(x):
        x1, x2, x3 = (p.astype(F32) for p in _split3(x))
        stacked = jnp.where(row16 == 0, x1, jnp.where(row16 == 1, x2, jnp.where(row16 == 2, x3, 0.0)))
        return stacked.astype(BF16)

    masks = [si <= ti, si >= ti]
    tris = [jnp.where(mk, 1.0, 0.0).astype(BF16) for mk in masks]
    edges = [L - 1, 0]
    units = [(j, d) for j in range(hp) for d in range(2)]
    ii_row, ff_row, state = {}, {}, {}
    for j, d in units:
        h = head_of(j)
        ii_row[j, d] = gate_row(d * 8 + h)
        ff_row[j, d] = gate_row(d * 8 + HEADS + h)
        if has_state:
            state[j, d] = (c0_ref[d, j].T, n0_ref[d, pl.ds(h, 1), :], m0_ref[d, pl.ds(h, 1), 0:1])
        else:
            state[j, d] = (jnp.zeros((hd, hd), F32), jnp.zeros((1, hd), F32), jnp.zeros((1, 1), F32))

    for step in range(nc):
        chunk = [step, nc - 1 - step]
        rows = {u: slice(chunk[u[1]] * L, (chunk[u[1]] + 1) * L) for u in units}
        b, g, g_col, g_src, mx = {}, {}, {}, {}, {}
        for u in units:
            b3 = _dot(stack3(ff_row[u][:, rows[u]]), tris[u[1]])
            b[u] = b3[0:1, :] + b3[1:2, :] + b3[2:3, :]
            g[u] = ii_row[u][:, rows[u]] - b[u]
        for u in units:
            g_col[u] = _dot_tn(stack3(g[u]), pick3)
        for u in units:
            g_src[u] = jnp.where(masks[u[1]], jnp.concatenate([g_col[u]] * (L // hd), axis=1), NEG_INF)
            mx[u] = jnp.maximum(state[u][2], jnp.max(g_src[u], axis=0, keepdims=True))
        for u in units:
            j, d = u
            hc = slice(j * hd, (j + 1) * hd)
            ct, n, m = state[u]
            qb = (q_ref[rows[u], hc].astype(F32) * scale).astype(BF16)
            k = k_ref[rows[u], hc].astype(F32)
            kb, vb = k.astype(BF16), v_ref[rows[u], hc].astype(BF16)
            p = _dot_nt(kb, qb) * jnp.exp(g_src[u] - mx[u])
            w_inter = jnp.exp(m - mx[u])
            n16 = jnp.broadcast_to(n, (16, hd)).astype(BF16)
            den = jnp.sum(p, axis=0, keepdims=True) + w_inter * _dot_nt(n16, qb)[0:1, :]
            num = _dot_tn(vb, p.astype(BF16)) + w_inter * _dot_nt(ct.astype(BF16), qb)
            ht_s[d, j, :, rows[u]] = num / jnp.maximum(jnp.abs(den), jnp.exp(-(b[u] + mx[u])))
            mx_end = mx[u][:, edges[d]:edges[d] + 1]
            w_prev = jnp.exp(m - mx_end)
            ku = k * jnp.exp(g_col[u] - mx_end)
            state[u] = (w_prev * ct + _dot_tn(vb, ku.astype(BF16)),
                        w_prev * n + jnp.sum(ku, axis=0, keepdims=True),
                        b[u][:, edges[d]:edges[d] + 1] + mx_end)

    if write_state:
        for j, d in units:
            ct, n, m = state[j, d]
            cout_ref[d, j] = ct.T
            nout_ref[d, pl.ds(head_of(j), 1), :] = n
            mout_ref[d, pl.ds(head_of(j), 1), :] = jnp.broadcast_to(m, (1, hd))

    for j in range(hp):
        hc = slice(j * hd, (j + 1) * hd)
        hs = (ht_s[0, j] + ht_s[1, j]).T
        y = hs * lax.rsqrt(jnp.mean(hs * hs, axis=-1, keepdims=True) + NORM_EPS) * onw_ref[:, hc]
        out_ref[:, hc] = (y * _sigmoid(og_ref[:, hc].astype(F32)) * _silu(g_ref[:, hc].astype(F32))).astype(BF16)


def _mlstm(p_main, p_if, batch, seq, gate_b, out_norm_w, hp, state=None, state_out=None):
    hd = HEAD_DIM
    nh = HEADS // hp
    w = hp * hd
    col = lambda base: (lambda b, h: (b, base * nh + h))
    in_specs = [pl.BlockSpec((seq, w), col(COL_ML_Q)), pl.BlockSpec((seq, w), col(COL_ML_K)),
                pl.BlockSpec((seq, w), col(COL_ML_V)), pl.BlockSpec((seq, w), col(COL_ML_O)),
                pl.BlockSpec((seq, w), col(COL_ML_G)),
                pl.BlockSpec((seq, 128), lambda b, h: (b, 0)),
                pl.BlockSpec((1, 128), lambda b, h: (0, 0)),
                pl.BlockSpec((1, w), lambda b, h: (0, h))]
    args = [p_main] * 5 + [p_if, jnp.pad(gate_b, (0, 128 - gate_b.shape[0])).reshape(1, 128),
                           out_norm_w.reshape(1, GROUP_W)]
    if state is not None:
        c0, n0, m0, layer = state
        m0b = jnp.broadcast_to(m0[..., None], m0.shape + (hd,))
        in_specs += [pl.BlockSpec((None, None, 2, hp, hd, hd), lambda b, h: (b, layer, 0, h, 0, 0)),
                     pl.BlockSpec((None, None, 2, HEADS, hd), lambda b, h: (b, layer, 0, 0, 0)),
                     pl.BlockSpec((None, None, 2, HEADS, hd), lambda b, h: (b, layer, 0, 0, 0))]
        args += [c0, n0, m0b]
    out_shape = [jax.ShapeDtypeStruct((batch * seq, GROUP_W), BF16)]
    out_specs = [pl.BlockSpec((seq, w), lambda b, h: (b, h))]
    aliases = {}
    if state_out is not None:
        c_all, layer_o = state_out
        aliases[len(args)] = 1
        in_specs.append(pl.BlockSpec(memory_space=pl.ANY))
        args.append(c_all)
        out_shape += [jax.ShapeDtypeStruct(c_all.shape, c_all.dtype),
                      jax.ShapeDtypeStruct((batch, 2, HEADS, hd), F32),
                      jax.ShapeDtypeStruct((batch, 2, HEADS, hd), F32)]
        out_specs += [pl.BlockSpec((None, None, 2, hp, hd, hd), lambda b, h: (b, layer_o, 0, h, 0, 0)),
                      pl.BlockSpec((None, 2, HEADS, hd), lambda b, h: (b, 0, 0, 0)),
                      pl.BlockSpec((None, 2, HEADS, hd), lambda b, h: (b, 0, 0, 0))]
    return pl.pallas_call(
        functools.partial(_mlstm_kernel, seq=seq, hp=hp, has_state=state is not None,
                          write_state=state_out is not None),
        out_shape=tuple(out_shape),
        grid=(batch, nh),
        in_specs=in_specs,
        out_specs=tuple(out_specs),
        scratch_shapes=[pltpu.VMEM((2, hp, hd, seq), F32)],
        input_output_aliases=aliases,
        compiler_params=_params("parallel", "arbitrary"),
        name="mlstm",
    )(*args)


class BiasPlan(NamedTuple):
    kinds: tuple
    kind_of: tuple
    key_start: tuple
    window: int


NA_Q_ROWS = 4
NA_K_ROWS = 12


def _na_bias(rpb, seq):
    rows = seq // GRID_W
    wr = min(NA_ROWS, rows)
    assert rows % NA_Q_ROWS == 0 and rows >= NA_K_ROWS
    c = np.arange(GRID_W)
    c_start = np.clip(c - NA_COLS // 2, 0, GRID_W - NA_COLS)
    col_ok = (c[None, :] >= c_start[:, None]) & (c[None, :] < c_start[:, None] + NA_COLS)
    col_off = np.clip(c[None, :] - c[:, None] + NA_COLS - 1, 0, 2 * NA_COLS - 2)
    col_sel = (col_off[..., None] == np.arange(2 * NA_COLS - 1)).astype(np.float32)
    blocks = jnp.einsum('lhij,cdj->lhicd', rpb.astype(F32), col_sel, precision=lax.Precision.HIGHEST)
    blocks = jnp.where(col_ok, blocks, NEG_INF)
    blocks = jnp.concatenate([blocks, jnp.full_like(blocks[:, :, :1], NEG_INF)], axis=2)
    masked = 2 * NA_ROWS - 1
    nblk = rows // NA_Q_ROWS
    key_row0 = [int(min(np.clip(i * NA_Q_ROWS - wr // 2, 0, rows - wr), rows - NA_K_ROWS)) for i in range(nblk)]
    idx = np.full((nblk, NA_Q_ROWS, NA_K_ROWS), masked, np.int32)
    for i in range(nblk):
        for a in range(NA_Q_ROWS):
            qr = i * NA_Q_ROWS + a
            r_start = int(np.clip(qr - wr // 2, 0, rows - wr))
            assert key_row0[i] <= r_start and r_start + wr <= key_row0[i] + NA_K_ROWS
            for kk in range(NA_K_ROWS):
                kr = key_row0[i] + kk
                if r_start <= kr < r_start + wr:
                    idx[i, a, kk] = kr - qr + NA_ROWS - 1
    kinds, kind_of = np.unique(idx.reshape(nblk, -1), axis=0, return_inverse=True)
    pair_of = kinds.reshape(-1, NA_Q_ROWS, NA_K_ROWS // 2, 2)
    pairs, entry = np.unique(pair_of.reshape(-1, 2), axis=0, return_inverse=True)
    table = jnp.concatenate([blocks[:, :, pairs[:, 0]], blocks[:, :, pairs[:, 1]]], axis=-1)
    entry = np.ravel(entry).reshape(pair_of.shape[:3])
    plan = BiasPlan(kinds=tuple(tuple(tuple(int(e) for e in row) for row in kind) for kind in entry),
                    kind_of=tuple(int(t) for t in np.ravel(kind_of)),
                    key_start=tuple(r * GRID_W for r in key_row0), window=NA_K_ROWS * GRID_W)
    return table, plan


def _rope_tables(seq):
    t = jnp.arange(seq)
    nf = HEAD_DIM // 4
    inv = ROPE_THETA ** (-jnp.arange(nf, dtype=F32) / nf)
    zeros = jnp.zeros((seq, nf), F32)

    def half(pos):
        ang = pos.astype(F32)[:, None] * inv
        cs, sn = jnp.cos(ang), jnp.sin(ang)
        return jnp.concatenate([cs, cs], -1), jnp.concatenate([-sn, zeros], -1), jnp.concatenate([zeros, sn], -1)

    parts = [half(t // GRID_W), half(t % GRID_W)]
    return tuple(jnp.concatenate([parts[0][i], parts[1][i]], -1) for i in range(3))


def kernel(x_prompt, x_sample, cache_na_k, cache_na_v, cache_gqa_k, cache_gqa_v, state_lru, state_mlstm_C,
           state_mlstm_n, state_mlstm_m, c, c_ctx, norm_w, ada_w, ada_b, w_in, lru_conv_w, lru_conv_b, lru_wr,
           lru_br, lru_wi, lru_bi, lru_lambda, na_rpb, gqa_qnorm, gqa_knorm, ml_gate_b, ml_out_norm, w_out,
           final_norm_w):
    bp, tp, d = x_prompt.shape
    bs, ts, _ = x_sample.shape
    depth = w_in.shape[0]
    past = cache_na_k.shape[2]

    cond = jnp.zeros((COND_ROWS, d), F32).at[0].set(c_ctx).at[1:1 + bs].set(c)
    mod4 = _modulation(cond, ada_w, ada_b).reshape(depth, COND_ROWS, 1, 3 * d)

    w_main = w_in[:, :, :MAIN_W].astype(BF16)
    w_if = jnp.pad(w_in[:, :, MAIN_W:], ((0, 0), (0, 0), (0, 128 - (w_in.shape[2] - MAIN_W)))).astype(BF16)
    w_out_b = w_out.astype(BF16)
    w_gates = jnp.concatenate([lru_wr[:, 0], lru_wi[:, 0], lru_wr[:, 1], lru_wi[:, 1]], axis=-1).astype(BF16)
    bias_pairs, bias_plan = _na_bias(na_rpb, ts)
    rope = _rope_tables(ts)
    zeros_h0 = jnp.zeros((bp, 2, GROUP_W), F32)

    xp = x_prompt.reshape(bp * tp, d)
    xs = x_sample.reshape(bs * ts, d)
    norm_w2 = norm_w.reshape(depth, 1, d)
    fnw = final_norm_w.reshape(1, d)
    na_k = jnp.zeros((bp, depth, tp, HEADS, HEAD_DIM), F32)
    na_v = jnp.zeros((bp, depth, tp, HEADS, HEAD_DIM), F32)
    gqa_k = jnp.zeros((bp, depth, tp, GQA_KV_HEADS, HEAD_DIM), F32)
    gqa_v = jnp.zeros((bp, depth, tp, GQA_KV_HEADS, HEAD_DIM), F32)
    st_c = jnp.zeros((bp, depth, 2, HEADS, HEAD_DIM, HEAD_DIM), F32)
    new = {k: [] for k in ('lru', 'n', 'm')}

    for l in range(depth):
        last = l == depth - 1
        lru_args = (lru_conv_w[l], lru_conv_b[l].reshape(1, GROUP_W), w_gates[l], lru_br[l], lru_bi[l], lru_lambda[l])
        qk_norm = (gqa_qnorm[l].reshape(1, HEAD_DIM), gqa_knorm[l].reshape(1, HEAD_DIM))

        p, pif = _in_projection(xp, norm_w2[l], mod4, l, 0, bp * tp, w_main[l], w_if[l])
        out_a, st_lru = _lru(p, bp, tp, *lru_args, zeros_h0)
        out_b, na_k, na_v = _attention(p, bp, tp, col_q=COL_NA_Q, col_g=COL_NA_G, k_spec=COL_NA_K,
                                       v_spec=COL_NA_V, n_kv=HEADS, kv_out=(na_k, na_v, l))
        out_c, gqa_k, gqa_v = _attention(p, bp, tp, col_q=COL_GQA_Q, col_g=COL_GQA_G, k_spec=2 * COL_GQA_KV,
                                         v_spec=2 * COL_GQA_KV + 1, n_kv=GQA_KV_HEADS, qk_norm=qk_norm,
                                         kv_out=(gqa_k, gqa_v, l))
        out_d, st_c, st_n, st_m = _mlstm(p, pif, bp, tp, ml_gate_b[l], ml_out_norm[l], HEADS, state_out=(st_c, l))
        xp = _out_projection((out_a, out_b, out_c, out_d), w_out_b[l], xp, mod4, l, 0, bp * tp,
                             fnw if last else None)
        new['lru'].append(st_lru)
        new['n'].append(st_n)
        new['m'].append(st_m[..., 0])

        p, pif = _in_projection(xs, norm_w2[l], mod4, l, 1, ts, w_main[l], w_if[l])
        out_a, _ = _lru(p, bs, ts, *lru_args, state_lru[:, l])
        (out_b,) = _attention(p, bs, ts, col_q=COL_NA_Q, col_g=COL_NA_G, k_spec=COL_NA_K, v_spec=COL_NA_V,
                              n_kv=HEADS, cache=(cache_na_k, cache_na_v, l), bias=(bias_pairs, l, bias_plan))
        (out_c,) = _attention(p, bs, ts, col_q=COL_GQA_Q, col_g=COL_GQA_G, k_spec=2 * COL_GQA_KV,
                              v_spec=2 * COL_GQA_KV + 1, n_kv=GQA_KV_HEADS, cache=(cache_gqa_k, cache_gqa_v, l),
                              qk_norm=qk_norm, rope=rope)
        (out_d,) = _mlstm(p, pif, bs, ts, ml_gate_b[l], ml_out_norm[l], 2,
                          state=(state_mlstm_C, state_mlstm_n, state_mlstm_m, l))
        xs = _out_projection((out_a, out_b, out_c, out_d), w_out_b[l], xs, mod4, l, 1, ts,
                             fnw if last else None)

    stacked = {k: jnp.stack(v, axis=1) for k, v in new.items()}
    return (xp.reshape(bp, tp, d), xs.reshape(bs, ts, d), na_k, na_v, gqa_k, gqa_v,
            stacked['lru'], st_c, stacked['n'], stacked['m'])
```

```python
import functools
from typing import NamedTuple

import numpy as np
import jax
import jax.numpy as jnp
from jax import lax
from jax.experimental import pallas as pl
from jax.experimental.pallas import tpu as pltpu

F32 = jnp.float32
BF16 = jnp.bfloat16

HEAD_DIM = 128
GROUP_W = 512
HEADS = GROUP_W // HEAD_DIM
GQA_KV_HEADS = 2
GRID_W = 64
NORM_EPS = 1e-6
NEG_INF = -1e30
LOG2E = 1.4426950408889634
LN2 = 0.6931471805599453
LRU_C = 8.0
NA_ROWS, NA_COLS = 8, 16
ROPE_THETA = 10000.0
ML_CHUNK = 256
COND_ROWS = 16
ROW_CHUNK = 256
W_CAST_CHUNKS = 8
VMEM_LIMIT = 52 * 1024 * 1024

COL_LRU_X, COL_LRU_G, COL_NA_Q, COL_NA_K, COL_NA_V, COL_NA_G = 0, 1, 2, 3, 4, 5
COL_GQA_Q, COL_GQA_KV, COL_GQA_G = 6, 7, 8
COL_ML_Q, COL_ML_K, COL_ML_V, COL_ML_O, COL_ML_G = 9, 10, 11, 12, 13
MAIN_W = 14 * GROUP_W


def _params(*sem):
    return pltpu.CompilerParams(dimension_semantics=sem, vmem_limit_bytes=VMEM_LIMIT)


def _sigmoid(x):
    return 1.0 / (1.0 + jnp.exp(-x))


def _silu(x):
    return x * _sigmoid(x)


def _log_sigmoid(x):
    return jnp.minimum(x, 0.0) - jnp.log1p(jnp.exp(-jnp.abs(x)))


def _dot(a, b):
    return jnp.dot(a, b, preferred_element_type=F32)


def _dot_nt(a, b):
    return lax.dot_general(a, b, (((1,), (1,)), ((), ())), preferred_element_type=F32)


def _dot_tn(a, b):
    return lax.dot_general(a, b, (((0,), (0,)), ((), ())), preferred_element_type=F32)


def _split3(x):
    x1 = x.astype(BF16)
    r = x - x1.astype(F32)
    x2 = r.astype(BF16)
    x3 = (r - x2.astype(F32)).astype(BF16)
    return x1, x2, x3


def _mod_kernel(cond_ref, w_ref, b_ref, o_ref):
    s = _silu(cond_ref[...]).astype(BF16)
    o_ref[...] = _dot(s, w_ref[...].astype(BF16)) + b_ref[...]


def _modulation(cond, ada_w, ada_b):
    depth, d, n = ada_w.shape
    tn = 1024
    return pl.pallas_call(
        _mod_kernel,
        out_shape=jax.ShapeDtypeStruct((depth, COND_ROWS, n), F32),
        grid=(depth, n // tn),
        in_specs=[pl.BlockSpec((COND_ROWS, d), lambda l, j: (0, 0)),
                  pl.BlockSpec((None, d, tn), lambda l, j: (l, 0, j)),
                  pl.BlockSpec((None, 1, tn), lambda l, j: (l, 0, j))],
        out_specs=pl.BlockSpec((None, COND_ROWS, tn), lambda l, j: (l, 0, j)),
        compiler_params=_params("parallel", "parallel"),
        name="adaln_mod",
    )(cond, ada_w, ada_b.reshape(depth, 1, n))


def _inproj_kernel(x_ref, nw_ref, sh_ref, sc_ref, w_ref, wif_ref, o_ref, oif_ref, xm_ref):
    @pl.when(pl.program_id(1) == 0)
    def _():
        for r in range(x_ref.shape[0] // ROW_CHUNK):
            rows = slice(r * ROW_CHUNK, (r + 1) * ROW_CHUNK)
            x = x_ref[rows, :]
            y = x * lax.rsqrt(jnp.mean(x * x, axis=-1, keepdims=True) + NORM_EPS) * nw_ref[...]
            xm_ref[rows, :] = (y * (1.0 + sc_ref[...]) + sh_ref[...]).astype(BF16)
        oif_ref[...] = _dot(xm_ref[...], wif_ref[...])

    kc = w_ref.shape[0] // W_CAST_CHUNKS
    acc = _dot(xm_ref[:, 0:kc], w_ref[0:kc, :].astype(BF16))
    for c in range(1, W_CAST_CHUNKS):
        acc += _dot(xm_ref[:, c * kc:(c + 1) * kc], w_ref[c * kc:(c + 1) * kc, :].astype(BF16))
    o_ref[...] = acc.astype(o_ref.dtype)


def _in_projection(x, norm_w, mod4, layer, mod_row0, rows_per_mod, w_in, w_if):
    m, d = x.shape
    tm, tn = min(1024, m), 1024
    assert m % tm == 0 and rows_per_mod % tm == 0
    per_batch = rows_per_mod < m

    def mod_map(col):
        if per_batch:
            return lambda i, j: (layer, mod_row0 + (i * tm) // rows_per_mod, 0, col)
        return lambda i, j: (layer, mod_row0, 0, col)

    return pl.pallas_call(
        _inproj_kernel,
        out_shape=(jax.ShapeDtypeStruct((m, MAIN_W), BF16), jax.ShapeDtypeStruct((m, 128), F32)),
        grid=(m // tm, MAIN_W // tn),
        in_specs=[pl.BlockSpec((tm, d), lambda i, j: (i, 0)),
                  pl.BlockSpec((1, d), lambda i, j: (0, 0)),
                  pl.BlockSpec((None, None, 1, d), mod_map(0)),
                  pl.BlockSpec((None, None, 1, d), mod_map(1)),
                  pl.BlockSpec((None, d, tn), lambda i, j: (layer, 0, j)),
                  pl.BlockSpec((d, 128), lambda i, j: (0, 0))],
        out_specs=(pl.BlockSpec((tm, tn), lambda i, j: (i, j)),
                   pl.BlockSpec((tm, 128), lambda i, j: (i, 0))),
        scratch_shapes=[pltpu.VMEM((tm, d), BF16)],
        compiler_params=_params("parallel", "arbitrary"),
        name="in_proj",
    )(x, norm_w, mod4, mod4, w_in, w_if)


def _outproj_kernel(a_ref, b_ref, c_ref, d_ref, w_ref, x_ref, g_ref, *rest, final):
    o_ref = rest[-1]
    for n in range(w_ref.shape[1] // GROUP_W):
        cols = slice(n * GROUP_W, (n + 1) * GROUP_W)
        acc = _dot(a_ref[...], w_ref[0:GROUP_W, cols])
        acc += _dot(b_ref[...], w_ref[GROUP_W:2 * GROUP_W, cols])
        acc += _dot(c_ref[...], w_ref[2 * GROUP_W:3 * GROUP_W, cols])
        acc += _dot(d_ref[...], w_ref[3 * GROUP_W:4 * GROUP_W, cols])
        o_ref[:, cols] = x_ref[:, cols] + g_ref[:, cols] * acc
    if final:
        fnw_ref = rest[0]
        for r in range(o_ref.shape[0] // ROW_CHUNK):
            rows = slice(r * ROW_CHUNK, (r + 1) * ROW_CHUNK)
            xn = o_ref[rows, :]
            o_ref[rows, :] = xn * lax.rsqrt(jnp.mean(xn * xn, axis=-1, keepdims=True) + NORM_EPS) * fnw_ref[...]


def _out_projection(branches, w_out, x, mod4, layer, mod_row0, rows_per_mod, final_norm_w):
    m, d = x.shape
    tm = 512
    per_batch = rows_per_mod < m
    if per_batch:
        gate_map = lambda i: (layer, mod_row0 + (i * tm) // rows_per_mod, 0, 2)
    else:
        gate_map = lambda i: (layer, mod_row0, 0, 2)
    final = final_norm_w is not None
    in_specs = [pl.BlockSpec((tm, GROUP_W), lambda i: (i, 0)) for _ in range(4)]
    in_specs += [pl.BlockSpec((d, d), lambda i: (0, 0)),
                 pl.BlockSpec((tm, d), lambda i: (i, 0)),
                 pl.BlockSpec((None, None, 1, d), gate_map)]
    args = list(branches) + [w_out, x, mod4]
    if final:
        in_specs.append(pl.BlockSpec((1, d), lambda i: (0, 0)))
        args.append(final_norm_w)
    return pl.pallas_call(
        functools.partial(_outproj_kernel, final=final),
        out_shape=jax.ShapeDtypeStruct((m, d), F32),
        grid=(m // tm,),
        in_specs=in_specs,
        out_specs=pl.BlockSpec((tm, d), lambda i: (i, 0)),
        compiler_params=_params("parallel"),
        name="out_proj",
    )(*args)


def _scan8(a, b, row, reverse):
    for dist in (1, 2, 4):
        if reverse:
            keep = row < 8 - dist
            shift = 8 - dist
        else:
            keep = row >= dist
            shift = dist
        a_sh = jnp.where(keep, pltpu.roll(a, shift, 0), 1.0)
        b_sh = jnp.where(keep, pltpu.roll(b, shift, 0), 0.0)
        b = b + a * b_sh
        a = a * a_sh
    return a, b


def _lru_kernel(x_ref, g_ref, cw_ref, cb_ref, wg_ref, br_ref, bi_ref, lam_ref, h0_ref,
                o_ref, st_ref, xc_s, pre_s, a_s, b_s, h_s, *, seq):
    w = GROUP_W
    rc = 256
    x = x_ref[...].astype(F32)
    t = lax.broadcasted_iota(jnp.int32, (seq, w), 0)
    xc = x * cw_ref[1:2, :] + cb_ref[...]
    xc += jnp.where(t >= 1, pltpu.roll(x, 1, 0), 0.0) * cw_ref[0:1, :]
    xc += jnp.where(t < seq - 1, pltpu.roll(x, seq - 1, 0), 0.0) * cw_ref[2:3, :]
    xc += jnp.where(t < seq - 2, pltpu.roll(x, seq - 2, 0), 0.0) * cw_ref[3:4, :]
    xc_s[...] = xc
    for g in range(HEADS):
        pre = _dot(xc_s[:, g * 128:(g + 1) * 128].astype(BF16), wg_ref[g])
        for k in range(4):
            pre_s[k, :, g * 128:(g + 1) * 128] = pre[:, k * 128:(k + 1) * 128]
    for d in range(2):
        log_lam = LRU_C * _log_sigmoid(lam_ref[d:d + 1, :])
        for c in range(seq // rc):
            rows = slice(c * rc, (c + 1) * rc)
            r = _sigmoid(pre_s[2 * d, rows, :] + br_ref[d:d + 1, :])
            i = _sigmoid(pre_s[2 * d + 1, rows, :] + bi_ref[d:d + 1, :])
            log_a = log_lam * r
            a = jnp.exp(log_a)
            a_s[d, rows, :] = a
            b_s[d, rows, :] = jnp.sqrt(-jnp.tanh(log_a) * (a * a + 1.0)) * i * xc_s[rows, :]

    row = lax.broadcasted_iota(jnp.int32, (8, w), 0)
    ngroups = seq // 8

    def body(g, carry):
        hf, hb = carry
        rf = pl.multiple_of(g * 8, 8)
        rb = pl.multiple_of((ngroups - 1 - g) * 8, 8)
        af, bf = _scan8(a_s[0, pl.ds(rf, 8), :], b_s[0, pl.ds(rf, 8), :], row, False)
        ab, bb = _scan8(a_s[1, pl.ds(rb, 8), :], b_s[1, pl.ds(rb, 8), :], row, True)
        hf_new = bf + af * hf
        hb_new = bb + ab * hb
        h_s[0, pl.ds(rf, 8), :] = hf_new
        h_s[1, pl.ds(rb, 8), :] = hb_new
        return (jnp.broadcast_to(hf_new[7:8, :], (8, w)), jnp.broadcast_to(hb_new[0:1, :], (8, w)))

    hf0 = jnp.broadcast_to(h0_ref[0:1, :], (8, w))
    hb0 = jnp.broadcast_to(h0_ref[1:2, :], (8, w))
    hf_last, hb_last = lax.fori_loop(0, ngroups, body, (hf0, hb0))
    st_ref[0:1, :] = hf_last[0:1, :]
    st_ref[1:2, :] = hb_last[0:1, :]
    for c in range(seq // rc):
        rows = slice(c * rc, (c + 1) * rc)
        o_ref[rows, :] = ((h_s[0, rows, :] + h_s[1, rows, :]) * _silu(g_ref[rows, :].astype(F32))).astype(BF16)


def _lru(p_main, batch, seq, conv_w, conv_b, w_gates, b_r, b_i, lam, h0):
    w = GROUP_W
    const = lambda *shape: pl.BlockSpec(shape, lambda b: (0,) * len(shape))
    return pl.pallas_call(
        functools.partial(_lru_kernel, seq=seq),
        out_shape=(jax.ShapeDtypeStruct((batch * seq, w), BF16), jax.ShapeDtypeStruct((batch, 2, w), F32)),
        grid=(batch,),
        in_specs=[pl.BlockSpec((seq, w), lambda b: (b, COL_LRU_X)),
                  pl.BlockSpec((seq, w), lambda b: (b, COL_LRU_G)),
                  const(4, w), const(1, w), const(HEADS, 128, 4 * 128), const(2, w), const(2, w), const(2, w),
                  pl.BlockSpec((None, 2, w), lambda b: (b, 0, 0))],
        out_specs=(pl.BlockSpec((seq, w), lambda b: (b, 0)),
                   pl.BlockSpec((None, 2, w), lambda b: (b, 0, 0))),
        scratch_shapes=[pltpu.VMEM((seq, w), F32), pltpu.VMEM((4, seq, w), F32),
                        pltpu.VMEM((2, seq, w), F32), pltpu.VMEM((2, seq, w), F32), pltpu.VMEM((2, seq, w), F32)],
        compiler_params=_params("parallel"),
        name="rg_lru",
    )(p_main, p_main, conv_w, conv_b, w_gates, b_r, b_i, lam, h0)


def _head_rms(x, w):
    return x * lax.rsqrt(jnp.mean(x * x, axis=-1, keepdims=True) + NORM_EPS) * w


def _rope(x, cos, sin_lo, sin_hi):
    return x * cos + pltpu.roll(x, 96, 1) * sin_lo + pltpu.roll(x, 32, 1) * sin_hi


def _attn_kernel(*refs, n_kv, has_cache, bias_plan, qk_norm, rope, write_kv):
    has_bias = bias_plan is not None
    it = iter(refs)
    q_ref, k_ref, v_ref, g_ref = next(it), next(it), next(it), next(it)
    kc_ref = vc_ref = bias_ref = qn_ref = kn_ref = None
    if has_cache:
        kc_ref, vc_ref = next(it), next(it)
    if has_bias:
        bias_ref = next(it)
    if qk_norm:
        qn_ref, kn_ref = next(it), next(it)
    if rope:
        cq_ref, slq_ref, shq_ref, ck_ref, slk_ref, shk_ref = (next(it) for _ in range(6))
    if write_kv:
        next(it), next(it)
    o_ref = next(it)
    if write_kv:
        ko_ref, vo_ref = next(it), next(it)
    prep_k = qk_norm or rope
    if prep_k:
        kb_s = next(it)
    vb_s = next(it)
    if has_cache:
        kcb_s, vcb_s = next(it), next(it)
    if has_bias:
        bias_s = next(it)
        @pl.when(jnp.logical_and(pl.program_id(0) == 0, pl.program_id(1) == 0))
        def _():
            for t, kind in enumerate(bias_plan.kinds):
                for h in range(HEADS):
                    for a, row in enumerate(kind):
                        for p, entry in enumerate(row):
                            bias_s[t, h, a * GRID_W:(a + 1) * GRID_W, p * 128:(p + 1) * 128] = bias_ref[h, entry]

    @pl.when(pl.program_id(1) == 0)
    def _():
        for kvh in range(n_kv):
            cols = slice(kvh * HEAD_DIM, (kvh + 1) * HEAD_DIM)
            vcols = slice(2 * kvh * HEAD_DIM, (2 * kvh + 1) * HEAD_DIM)
            ocols = slice((2 * kvh + 1) * HEAD_DIM, (2 * kvh + 2) * HEAD_DIM)
            k = k_ref[:, cols]
            v = v_ref[:, cols]
            if qk_norm:
                k = _head_rms(k.astype(F32), kn_ref[...])
            if write_kv:
                ko_ref[pl.ds(kvh, k.shape[0], stride=n_kv), :] = k.astype(F32)
                vo_ref[pl.ds(kvh, v.shape[0], stride=n_kv), :] = v.astype(F32)
            if rope:
                k = _rope(k.astype(F32), ck_ref[...], slk_ref[...], shk_ref[...])
            if prep_k:
                kb_s[:, cols] = k.astype(BF16)
            vb_s[:, vcols] = v.astype(BF16)
            vb_s[:, ocols] = jnp.ones((vb_s.shape[0], HEAD_DIM), BF16)
            if has_cache:
                lc = kcb_s.shape[0]
                kcb_s[:, cols] = kc_ref[pl.ds(kvh, lc, stride=n_kv), :].astype(BF16)
                vcb_s[:, vcols] = vc_ref[pl.ds(kvh, lc, stride=n_kv), :].astype(BF16)
                vcb_s[:, ocols] = jnp.ones((vcb_s.shape[0], HEAD_DIM), BF16)

    c2 = HEAD_DIM ** -0.5 * LOG2E
    group = HEADS // n_kv
    krows = slice(None)
    if has_bias:
        start, kind_id = 0, 0
        for blk in range(len(bias_plan.kind_of)):
            start = jnp.where(pl.program_id(1) == blk, bias_plan.key_start[blk], start)
            kind_id = jnp.where(pl.program_id(1) == blk, bias_plan.kind_of[blk], kind_id)
        krows = pl.ds(pl.multiple_of(start, 256), bias_plan.window)
    for kvh in range(n_kv):
        cols = slice(kvh * HEAD_DIM, (kvh + 1) * HEAD_DIM)
        v1cols = slice(2 * kvh * HEAD_DIM, (2 * kvh + 2) * HEAD_DIM)
        kb = kb_s[krows, cols] if prep_k else k_ref[krows, cols]
        v1 = vb_s[krows, v1cols]
        for gi in range(group):
            h = kvh * group + gi
            hc = slice(h * HEAD_DIM, (h + 1) * HEAD_DIM)
            q = q_ref[:, hc]
            if qk_norm:
                q = _head_rms(q.astype(F32), qn_ref[...])
            if rope:
                q = _rope(q.astype(F32), cq_ref[...], slq_ref[...], shq_ref[...])
            qb = q.astype(BF16)
            s1 = _dot_nt(qb, kb)
            if has_cache:
                s2 = _dot_nt(qb, kcb_s[:, cols])
            if has_bias:
                s1 = s1 * c2 + bias_s[kind_id, h]
                mx = jnp.max(s1, axis=-1, keepdims=True)
                if has_cache:
                    s2 = s2 * c2
                    mx = jnp.maximum(mx, jnp.max(s2, axis=-1, keepdims=True))
                p1 = jnp.exp2(s1 - mx)
                p2 = jnp.exp2(s2 - mx) if has_cache else None
            else:
                mx = jnp.max(s1, axis=-1, keepdims=True)
                if has_cache:
                    mx = jnp.maximum(mx, jnp.max(s2, axis=-1, keepdims=True))
                p1 = jnp.exp2((s1 - mx) * c2)
                p2 = jnp.exp2((s2 - mx) * c2) if has_cache else None
            acc = _dot(p1.astype(BF16), v1)
            if has_cache:
                acc = acc + _dot(p2.astype(BF16), vcb_s[:, v1cols])
            o = acc[:, 0:HEAD_DIM] / acc[:, HEAD_DIM:HEAD_DIM + 1]
            o_ref[:, hc] = (o * _silu(g_ref[:, hc].astype(F32))).astype(BF16)


def _attention(p_main, batch, seq, *, col_q, col_g, k_spec, v_spec, n_kv, cache=None, bias=None,
               qk_norm=None, rope=None, kv_out=None):
    tq = 256
    nq = seq // tq
    kvw = n_kv * HEAD_DIM
    in_specs = [pl.BlockSpec((tq, GROUP_W), lambda b, i: (b * nq + i, col_q)),
                pl.BlockSpec((seq, kvw), lambda b, i: (b, k_spec)),
                pl.BlockSpec((seq, kvw), lambda b, i: (b, v_spec)),
                pl.BlockSpec((tq, GROUP_W), lambda b, i: (b * nq + i, col_g))]
    args = [p_main, p_main, p_main, p_main]
    scratch = [pltpu.VMEM((seq, kvw), BF16)] if (qk_norm is not None or rope is not None) else []
    scratch.append(pltpu.VMEM((seq, 2 * kvw), BF16))
    if cache is not None:
        kc, vc, layer = cache
        lc = kc.shape[2] // n_kv
        in_specs += [pl.BlockSpec((None, None, lc * n_kv, HEAD_DIM), lambda b, i: (b, layer, 0, 0))] * 2
        args += [kc, vc]
        scratch += [pltpu.VMEM((lc, kvw), BF16), pltpu.VMEM((lc, 2 * kvw), BF16)]
    bias_plan = None
    if bias is not None:
        pairs, layer_b, bias_plan = bias
        in_specs.append(pl.BlockSpec((None,) + pairs.shape[1:], lambda b, i: (layer_b, 0, 0, 0, 0)))
        args.append(pairs)
        scratch.append(pltpu.VMEM((len(bias_plan.kinds), HEADS, tq, bias_plan.window), F32))
    if qk_norm is not None:
        in_specs += [pl.BlockSpec((1, HEAD_DIM), lambda b, i: (0, 0))] * 2
        args += list(qk_norm)
    if rope is not None:
        in_specs += [pl.BlockSpec((tq, HEAD_DIM), lambda b, i: (i, 0))] * 3
        in_specs += [pl.BlockSpec((seq, HEAD_DIM), lambda b, i: (0, 0))] * 3
        args += list(rope) + list(rope)
    out_shape = [jax.ShapeDtypeStruct((batch * seq, GROUP_W), BF16)]
    out_specs = [pl.BlockSpec((tq, GROUP_W), lambda b, i: (b * nq + i, 0))]
    aliases = {}
    if kv_out is not None:
        assert nq == 1
        k_all, v_all, layer_o = kv_out
        for j, arr in enumerate((k_all, v_all)):
            aliases[len(args)] = 1 + j
            in_specs.append(pl.BlockSpec(memory_space=pl.ANY))
            args.append(arr)
            out_shape.append(jax.ShapeDtypeStruct(arr.shape, arr.dtype))
            out_specs.append(pl.BlockSpec((None, None, seq * n_kv, HEAD_DIM), lambda b, i: (b, layer_o, 0, 0)))
    return pl.pallas_call(
        functools.partial(_attn_kernel, n_kv=n_kv, has_cache=cache is not None, bias_plan=bias_plan,
                          qk_norm=qk_norm is not None, rope=rope is not None, write_kv=kv_out is not None),
        out_shape=tuple(out_shape),
        grid=(batch, nq),
        in_specs=in_specs,
        out_specs=tuple(out_specs),
        scratch_shapes=scratch,
        input_output_aliases=aliases,
        compiler_params=_params("arbitrary", "arbitrary"),
        name="attention",
    )(*args)


def _mlstm_kernel(*refs, seq, hp, has_state, write_state):
    it = iter(refs)
    q_ref, k_ref, v_ref, og_ref, g_ref, gates_ref, gb_ref, onw_ref = (next(it) for _ in range(8))
    if has_state:
        c0_ref, n0_ref, m0_ref = next(it), next(it), next(it)
    if write_state:
        next(it)
    out_ref = next(it)
    if write_state:
        cout_ref, nout_ref, mout_ref = next(it), next(it), next(it)
    ht_s = next(it)

    L = ML_CHUNK
    nc = seq // L
    hd = HEAD_DIM
    scale = hd ** -0.5
    gt = (gates_ref[...] + gb_ref[...]).T[0:4 * HEADS, :]
    gate_id = lax.broadcasted_iota(jnp.int32, gt.shape, 0)
    gt = jnp.where((gate_id // HEADS) % 2 == 1, _log_sigmoid(gt), gt) * LOG2E

    def head_of(j):
        return j if hp == HEADS else pl.program_id(1) * hp + j

    def gate_row(r):
        if isinstance(r, int):
            return gt[r:r + 1, :]
        return jnp.sum(jnp.where(gate_id == r, gt, 0.0), axis=0, keepdims=True)

    si = lax.broadcasted_iota(jnp.int32, (L, L), 0)
    ti = lax.broadcasted_iota(jnp.int32, (L, L), 1)
    pick3 = jnp.where(lax.broadcasted_iota(jnp.int32, (16, hd), 0) < 3, 1.0, 0.0).astype(BF16)
    row16 = lax.broadcasted_iota(jnp.int32, (16, L), 0)

    def stack3(x):
        x1, x2, x3 = (p.astype(F32) for p in _split3(x))
        stacked = jnp.where(row16 == 0, x1, jnp.where(row16 == 1, x2, jnp.where(row16 == 2, x3, 0.0)))
        return stacked.astype(BF16)

    masks = [si <= ti, si >= ti]
    tris = [jnp.where(mk, 1.0, 0.0).astype(BF16) for mk in masks]
    edges = [L - 1, 0]
    units = [(j, d) for j in range(hp) for d in range(2)]
    ii_row, ff_row, state = {}, {}, {}
    for j, d in units:
        h = head_of(j)
        ii_row[j, d] = gate_row(d * 8 + h)
        ff_row[j, d] = gate_row(d * 8 + HEADS + h)
        if has_state:
            state[j, d] = (c0_ref[d, j].T, n0_ref[d, pl.ds(h, 1), :], m0_ref[d, pl.ds(h, 1), 0:1] * LOG2E)
        else:
            state[j, d] = (jnp.zeros((hd, hd), F32), jnp.zeros((1, hd), F32), jnp.zeros((1, 1), F32))

    ones_cols = jnp.ones((L, hd), BF16)

    def load_chunk(j, c):
        hc = slice(j * hd, (j + 1) * hd)
        rws = slice(c * L, (c + 1) * L)
        k = k_ref[rws, hc].astype(F32)
        return ((q_ref[rws, hc].astype(F32) * scale).astype(BF16), k, k.astype(BF16),
                jnp.concatenate([v_ref[rws, hc].astype(BF16), ones_cols], axis=1))

    loaded = {(j, c): load_chunk(j, c) for j in range(hp) for c in range(nc)}
    for step in range(nc):
        chunk = [step, nc - 1 - step]
        rows = {u: slice(chunk[u[1]] * L, (chunk[u[1]] + 1) * L) for u in units}
        b, g, g_col, g_src, mx = {}, {}, {}, {}, {}
        for u in units:
            b3 = _dot(stack3(ff_row[u][:, rows[u]]), tris[u[1]])
            b[u] = b3[0:1, :] + b3[1:2, :] + b3[2:3, :]
            g[u] = ii_row[u][:, rows[u]] - b[u]
        for u in units:
            g_col[u] = _dot_tn(stack3(g[u]), pick3)
        for u in units:
            g_src[u] = jnp.where(masks[u[1]], jnp.concatenate([g_col[u]] * (L // hd), axis=1), NEG_INF)
            mx[u] = jnp.maximum(state[u][2], jnp.max(g_src[u], axis=0, keepdims=True))
        for u in units:
            j, d = u
            ct, n, m = state[u]
            qb, k, kb, v1 = loaded[j, chunk[d]]
            p = _dot_nt(kb, qb) * jnp.exp2(g_src[u] - mx[u])
            w_inter = jnp.exp2(m - mx[u])
            n16 = jnp.broadcast_to(n, (16, hd)).astype(BF16)
            pv = _dot_tn(v1, p.astype(BF16))
            den = pv[hd:hd + 1, :] + w_inter * _dot_nt(n16, qb)[0:1, :]
            num = pv[0:hd, :] + w_inter * _dot_nt(ct.astype(BF16), qb)
            ht_s[d, j, :, rows[u]] = num / jnp.maximum(jnp.abs(den), jnp.exp2(-(b[u] + mx[u])))
            mx_end = mx[u][:, edges[d]:edges[d] + 1]
            w_prev = jnp.exp2(m - mx_end)
            ku = k * jnp.exp2(g_col[u] - mx_end)
            kv = _dot_tn(v1, ku.astype(BF16))
            state[u] = (w_prev * ct + kv[0:hd, :], w_prev * n + kv[hd:hd + 1, :],
                        b[u][:, edges[d]:edges[d] + 1] + mx_end)

    if write_state:
        for j, d in units:
            ct, n, m = state[j, d]
            cout_ref[d, j] = ct.T
            nout_ref[d, pl.ds(head_of(j), 1), :] = n
            mout_ref[d, pl.ds(head_of(j), 1), :] = jnp.broadcast_to(m * LN2, (1, hd))

    for j in range(hp):
        hc = slice(j * hd, (j + 1) * hd)
        hs = (ht_s[0, j] + ht_s[1, j]).T
        y = hs * lax.rsqrt(jnp.mean(hs * hs, axis=-1, keepdims=True) + NORM_EPS) * onw_ref[:, hc]
        out_ref[:, hc] = (y * _sigmoid(og_ref[:, hc].astype(F32)) * _silu(g_ref[:, hc].astype(F32))).astype(BF16)


def _mlstm(p_main, p_if, batch, seq, gate_b, out_norm_w, hp, state=None, state_out=None):
    hd = HEAD_DIM
    nh = HEADS // hp
    w = hp * hd
    col = lambda base: (lambda b, h: (b, base * nh + h))
    in_specs = [pl.BlockSpec((seq, w), col(COL_ML_Q)), pl.BlockSpec((seq, w), col(COL_ML_K)),
                pl.BlockSpec((seq, w), col(COL_ML_V)), pl.BlockSpec((seq, w), col(COL_ML_O)),
                pl.BlockSpec((seq, w), col(COL_ML_G)),
                pl.BlockSpec((seq, 128), lambda b, h: (b, 0)),
                pl.BlockSpec((1, 128), lambda b, h: (0, 0)),
                pl.BlockSpec((1, w), lambda b, h: (0, h))]
    args = [p_main] * 5 + [p_if, jnp.pad(gate_b, (0, 128 - gate_b.shape[0])).reshape(1, 128),
                           out_norm_w.reshape(1, GROUP_W)]
    if state is not None:
        c0, n0, m0, layer = state
        m0b = jnp.broadcast_to(m0[..., None], m0.shape + (hd,))
        in_specs += [pl.BlockSpec((None, None, 2, hp, hd, hd), lambda b, h: (b, layer, 0, h, 0, 0)),
                     pl.BlockSpec((None, None, 2, HEADS, hd), lambda b, h: (b, layer, 0, 0, 0)),
                     pl.BlockSpec((None, None, 2, HEADS, hd), lambda b, h: (b, layer, 0, 0, 0))]
        args += [c0, n0, m0b]
    out_shape = [jax.ShapeDtypeStruct((batch * seq, GROUP_W), BF16)]
    out_specs = [pl.BlockSpec((seq, w), lambda b, h: (b, h))]
    aliases = {}
    if state_out is not None:
        c_all, layer_o = state_out
        aliases[len(args)] = 1
        in_specs.append(pl.BlockSpec(memory_space=pl.ANY))
        args.append(c_all)
        out_shape += [jax.ShapeDtypeStruct(c_all.shape, c_all.dtype),
                      jax.ShapeDtypeStruct((batch, 2, HEADS, hd), F32),
                      jax.ShapeDtypeStruct((batch, 2, HEADS, hd), F32)]
        out_specs += [pl.BlockSpec((None, None, 2, hp, hd, hd), lambda b, h: (b, layer_o, 0, h, 0, 0)),
                      pl.BlockSpec((None, 2, HEADS, hd), lambda b, h: (b, 0, 0, 0)),
                      pl.BlockSpec((None, 2, HEADS, hd), lambda b, h: (b, 0, 0, 0))]
    return pl.pallas_call(
        functools.partial(_mlstm_kernel, seq=seq, hp=hp, has_state=state is not None,
                          write_state=state_out is not None),
        out_shape=tuple(out_shape),
        grid=(batch, nh),
        in_specs=in_specs,
        out_specs=tuple(out_specs),
        scratch_shapes=[pltpu.VMEM((2, hp, hd, seq), F32)],
        input_output_aliases=aliases,
        compiler_params=_params("parallel", "arbitrary"),
        name="mlstm",
    )(*args)


class BiasPlan(NamedTuple):
    kinds: tuple
    kind_of: tuple
    key_start: tuple
    window: int


NA_Q_ROWS = 4
NA_K_ROWS = 12


def _na_bias(rpb, seq):
    rows = seq // GRID_W
    wr = min(NA_ROWS, rows)
    assert rows % NA_Q_ROWS == 0 and rows >= NA_K_ROWS
    c = np.arange(GRID_W)
    c_start = np.clip(c - NA_COLS // 2, 0, GRID_W - NA_COLS)
    col_ok = (c[None, :] >= c_start[:, None]) & (c[None, :] < c_start[:, None] + NA_COLS)
    col_off = np.clip(c[None, :] - c[:, None] + NA_COLS - 1, 0, 2 * NA_COLS - 2)
    col_sel = (col_off[..., None] == np.arange(2 * NA_COLS - 1)).astype(np.float32)
    blocks = jnp.einsum('lhij,cdj->lhicd', rpb.astype(F32), col_sel, precision=lax.Precision.HIGHEST)
    blocks = jnp.where(col_ok, blocks * LOG2E, NEG_INF)
    blocks = jnp.concatenate([blocks, jnp.full_like(blocks[:, :, :1], NEG_INF)], axis=2)
    masked = 2 * NA_ROWS - 1
    nblk = rows // NA_Q_ROWS
    key_row0 = [int(min(np.clip(i * NA_Q_ROWS - wr // 2, 0, rows - wr), rows - NA_K_ROWS)) for i in range(nblk)]
    idx = np.full((nblk, NA_Q_ROWS, NA_K_ROWS), masked, np.int32)
    for i in range(nblk):
        for a in range(NA_Q_ROWS):
            qr = i * NA_Q_ROWS + a
            r_start = int(np.clip(qr - wr // 2, 0, rows - wr))
            assert key_row0[i] <= r_start and r_start + wr <= key_row0[i] + NA_K_ROWS
            for kk in range(NA_K_ROWS):
                kr = key_row0[i] + kk
                if r_start <= kr < r_start + wr:
                    idx[i, a, kk] = kr - qr + NA_ROWS - 1
    kinds, kind_of = np.unique(idx.reshape(nblk, -1), axis=0, return_inverse=True)
    pair_of = kinds.reshape(-1, NA_Q_ROWS, NA_K_ROWS // 2, 2)
    pairs, entry = np.unique(pair_of.reshape(-1, 2), axis=0, return_inverse=True)
    table = jnp.concatenate([blocks[:, :, pairs[:, 0]], blocks[:, :, pairs[:, 1]]], axis=-1)
    entry = np.ravel(entry).reshape(pair_of.shape[:3])
    plan = BiasPlan(kinds=tuple(tuple(tuple(int(e) for e in row) for row in kind) for kind in entry),
                    kind_of=tuple(int(t) for t in np.ravel(kind_of)),
                    key_start=tuple(r * GRID_W for r in key_row0), window=NA_K_ROWS * GRID_W)
    return table, plan


def _rope_tables(seq):
    t = jnp.arange(seq)
    nf = HEAD_DIM // 4
    inv = ROPE_THETA ** (-jnp.arange(nf, dtype=F32) / nf)
    zeros = jnp.zeros((seq, nf), F32)

    def half(pos):
        ang = pos.astype(F32)[:, None] * inv
        cs, sn = jnp.cos(ang), jnp.sin(ang)
        return jnp.concatenate([cs, cs], -1), jnp.concatenate([-sn, zeros], -1), jnp.concatenate([zeros, sn], -1)

    parts = [half(t // GRID_W), half(t % GRID_W)]
    return tuple(jnp.concatenate([parts[0][i], parts[1][i]], -1) for i in range(3))


def kernel(x_prompt, x_sample, cache_na_k, cache_na_v, cache_gqa_k, cache_gqa_v, state_lru, state_mlstm_C,
           state_mlstm_n, state_mlstm_m, c, c_ctx, norm_w, ada_w, ada_b, w_in, lru_conv_w, lru_conv_b, lru_wr,
           lru_br, lru_wi, lru_bi, lru_lambda, na_rpb, gqa_qnorm, gqa_knorm, ml_gate_b, ml_out_norm, w_out,
           final_norm_w):
    bp, tp, d = x_prompt.shape
    bs, ts, _ = x_sample.shape
    depth = w_in.shape[0]
    past = cache_na_k.shape[2]

    cond = jnp.zeros((COND_ROWS, d), F32).at[0].set(c_ctx).at[1:1 + bs].set(c)
    mod4 = _modulation(cond, ada_w, ada_b).reshape(depth, COND_ROWS, 1, 3 * d)

    w_if = jnp.pad(w_in[:, :, MAIN_W:], ((0, 0), (0, 0), (0, 128 - (w_in.shape[2] - MAIN_W)))).astype(BF16)
    w_out_b = w_out.astype(BF16)
    w_gates = jnp.concatenate([lru_wr[:, 0], lru_wi[:, 0], lru_wr[:, 1], lru_wi[:, 1]], axis=-1).astype(BF16)
    bias_pairs, bias_plan = _na_bias(na_rpb, ts)
    rope = _rope_tables(ts)
    zeros_h0 = jnp.zeros((bp, 2, GROUP_W), F32)

    xp = x_prompt.reshape(bp * tp, d)
    xs = x_sample.reshape(bs * ts, d)
    norm_w2 = norm_w.reshape(depth, 1, d)
    fnw = final_norm_w.reshape(1, d)
    na_k = jnp.zeros((bp, depth, tp * HEADS, HEAD_DIM), F32)
    na_v = jnp.zeros((bp, depth, tp * HEADS, HEAD_DIM), F32)
    gqa_k = jnp.zeros((bp, depth, tp * GQA_KV_HEADS, HEAD_DIM), F32)
    gqa_v = jnp.zeros((bp, depth, tp * GQA_KV_HEADS, HEAD_DIM), F32)
    st_c = jnp.zeros((bp, depth, 2, HEADS, HEAD_DIM, HEAD_DIM), F32)
    flat = lambda a: a.reshape(a.shape[0], a.shape[1], a.shape[2] * a.shape[3], a.shape[4])
    cache_na = (flat(cache_na_k), flat(cache_na_v))
    cache_gqa = (flat(cache_gqa_k), flat(cache_gqa_v))
    new = {k: [] for k in ('lru', 'n', 'm')}

    for l in range(depth):
        last = l == depth - 1
        lru_args = (lru_conv_w[l], lru_conv_b[l].reshape(1, GROUP_W), w_gates[l], lru_br[l], lru_bi[l], lru_lambda[l])
        qk_norm = (gqa_qnorm[l].reshape(1, HEAD_DIM), gqa_knorm[l].reshape(1, HEAD_DIM))

        p, pif = _in_projection(xp, norm_w2[l], mod4, l, 0, bp * tp, w_in, w_if[l])
        out_a, st_lru = _lru(p, bp, tp, *lru_args, zeros_h0)
        out_b, na_k, na_v = _attention(p, bp, tp, col_q=COL_NA_Q, col_g=COL_NA_G, k_spec=COL_NA_K,
                                       v_spec=COL_NA_V, n_kv=HEADS, kv_out=(na_k, na_v, l))
        out_c, gqa_k, gqa_v = _attention(p, bp, tp, col_q=COL_GQA_Q, col_g=COL_GQA_G, k_spec=2 * COL_GQA_KV,
                                         v_spec=2 * COL_GQA_KV + 1, n_kv=GQA_KV_HEADS, qk_norm=qk_norm,
                                         kv_out=(gqa_k, gqa_v, l))
        out_d, st_c, st_n, st_m = _mlstm(p, pif, bp, tp, ml_gate_b[l], ml_out_norm[l], HEADS, state_out=(st_c, l))
        xp = _out_projection((out_a, out_b, out_c, out_d), w_out_b[l], xp, mod4, l, 0, bp * tp,
                             fnw if last else None)
        new['lru'].append(st_lru)
        new['n'].append(st_n)
        new['m'].append(st_m[..., 0])

        p, pif = _in_projection(xs, norm_w2[l], mod4, l, 1, ts, w_in, w_if[l])
        out_a, _ = _lru(p, bs, ts, *lru_args, state_lru[:, l])
        (out_b,) = _attention(p, bs, ts, col_q=COL_NA_Q, col_g=COL_NA_G, k_spec=COL_NA_K, v_spec=COL_NA_V,
                              n_kv=HEADS, cache=cache_na + (l,), bias=(bias_pairs, l, bias_plan))
        (out_c,) = _attention(p, bs, ts, col_q=COL_GQA_Q, col_g=COL_GQA_G, k_spec=2 * COL_GQA_KV,
                              v_spec=2 * COL_GQA_KV + 1, n_kv=GQA_KV_HEADS, cache=cache_gqa + (l,),
                              qk_norm=qk_norm, rope=rope)
        (out_d,) = _mlstm(p, pif, bs, ts, ml_gate_b[l], ml_out_norm[l], HEADS,
                          state=(state_mlstm_C, state_mlstm_n, state_mlstm_m, l))
        xs = _out_projection((out_a, out_b, out_c, out_d), w_out_b[l], xs, mod4, l, 1, ts,
                             fnw if last else None)

    stacked = {k: jnp.stack(v, axis=1) for k, v in new.items()}
    heads = lambda a, n: a.reshape(bp, depth, tp, n, HEAD_DIM)
    return (xp.reshape(bp, tp, d), xs.reshape(bs, ts, d), heads(na_k, HEADS), heads(na_v, HEADS),
            heads(gqa_k, GQA_KV_HEADS), heads(gqa_v, GQA_KV_HEADS),
            stacked['lru'], st_c, stacked['n'], stacked['m'])
```

```python
import functools
from typing import NamedTuple

import numpy as np
import jax
import jax.numpy as jnp
from jax import lax
from jax.experimental import pallas as pl
from jax.experimental.pallas import tpu as pltpu

F32 = jnp.float32
BF16 = jnp.bfloat16

HEAD_DIM = 128
GROUP_W = 512
HEADS = GROUP_W // HEAD_DIM
GQA_KV_HEADS = 2
GRID_W = 64
NORM_EPS = 1e-6
NEG_INF = -1e30
LOG2E = 1.4426950408889634
LN2 = 0.6931471805599453
LRU_C = 8.0
NA_ROWS, NA_COLS = 8, 16
ROPE_THETA = 10000.0
ML_CHUNK = 256
COND_ROWS = 16
ROW_CHUNK = 256
W_CAST_CHUNKS = 8
VMEM_LIMIT = 52 * 1024 * 1024

COL_LRU_X, COL_LRU_G, COL_NA_Q, COL_NA_K, COL_NA_V, COL_NA_G = 0, 1, 2, 3, 4, 5
COL_GQA_Q, COL_GQA_KV, COL_GQA_G = 6, 7, 8
COL_ML_Q, COL_ML_K, COL_ML_V, COL_ML_O, COL_ML_G = 9, 10, 11, 12, 13
MAIN_W = 14 * GROUP_W


def _params(*sem):
    return pltpu.CompilerParams(dimension_semantics=sem, vmem_limit_bytes=VMEM_LIMIT)


def _sigmoid(x):
    return 1.0 / (1.0 + jnp.exp(-x))


def _silu(x):
    return x * _sigmoid(x)


def _log_sigmoid(x):
    return jnp.minimum(x, 0.0) - jnp.log1p(jnp.exp(-jnp.abs(x)))


def _dot(a, b):
    return jnp.dot(a, b, preferred_element_type=F32)


def _dot_nt(a, b):
    return lax.dot_general(a, b, (((1,), (1,)), ((), ())), preferred_element_type=F32)


def _dot_tn(a, b):
    return lax.dot_general(a, b, (((0,), (0,)), ((), ())), preferred_element_type=F32)


def _split3(x):
    x1 = x.astype(BF16)
    r = x - x1.astype(F32)
    x2 = r.astype(BF16)
    x3 = (r - x2.astype(F32)).astype(BF16)
    return x1, x2, x3


def _mod_kernel(cond_ref, w_ref, b_ref, o_ref):
    s = _silu(cond_ref[...]).astype(BF16)
    o_ref[...] = _dot(s, w_ref[...].astype(BF16)) + b_ref[...]


def _modulation(cond, ada_w, ada_b):
    depth, d, n = ada_w.shape
    tn = 1024
    return pl.pallas_call(
        _mod_kernel,
        out_shape=jax.ShapeDtypeStruct((depth, COND_ROWS, n), F32),
        grid=(depth, n // tn),
        in_specs=[pl.BlockSpec((COND_ROWS, d), lambda l, j: (0, 0)),
                  pl.BlockSpec((None, d, tn), lambda l, j: (l, 0, j)),
                  pl.BlockSpec((None, 1, tn), lambda l, j: (l, 0, j))],
        out_specs=pl.BlockSpec((None, COND_ROWS, tn), lambda l, j: (l, 0, j)),
        compiler_params=_params("parallel", "parallel"),
        name="adaln_mod",
    )(cond, ada_w, ada_b.reshape(depth, 1, n))


def _inproj_kernel(x_ref, nw_ref, sh_ref, sc_ref, w_ref, wif_ref, o_ref, oif_ref, xm_ref, *, n_if):
    @pl.when(pl.program_id(1) == 0)
    def _():
        for r in range(x_ref.shape[0] // ROW_CHUNK):
            rows = slice(r * ROW_CHUNK, (r + 1) * ROW_CHUNK)
            x = x_ref[rows, :]
            y = x * lax.rsqrt(jnp.mean(x * x, axis=-1, keepdims=True) + NORM_EPS) * nw_ref[...]
            xm_ref[rows, :] = (y * (1.0 + sc_ref[...]) + sh_ref[...]).astype(BF16)
        wif = jnp.concatenate([wif_ref[...], jnp.zeros((128 - n_if, wif_ref.shape[1]), F32)], axis=0)
        oif_ref[...] = _dot_nt(xm_ref[...], wif.astype(BF16))

    kc = w_ref.shape[1] // W_CAST_CHUNKS
    acc = _dot_nt(xm_ref[:, 0:kc], w_ref[:, 0:kc].astype(BF16))
    for c in range(1, W_CAST_CHUNKS):
        acc += _dot_nt(xm_ref[:, c * kc:(c + 1) * kc], w_ref[:, c * kc:(c + 1) * kc].astype(BF16))
    o_ref[...] = acc.astype(o_ref.dtype)


def _in_projection(x, norm_w, mod4, layer, mod_row0, rows_per_mod, w_in_t):
    m, d = x.shape
    tm, tn = min(1024, m), 1024
    n_if = w_in_t.shape[1] - MAIN_W
    assert m % tm == 0 and rows_per_mod % tm == 0 and n_if % 8 == 0 and MAIN_W % n_if == 0 and n_if <= 128
    per_batch = rows_per_mod < m

    def mod_map(col):
        if per_batch:
            return lambda i, j: (layer, mod_row0 + (i * tm) // rows_per_mod, 0, col)
        return lambda i, j: (layer, mod_row0, 0, col)

    return pl.pallas_call(
        functools.partial(_inproj_kernel, n_if=n_if),
        out_shape=(jax.ShapeDtypeStruct((m, MAIN_W), BF16), jax.ShapeDtypeStruct((m, 128), F32)),
        grid=(m // tm, MAIN_W // tn),
        in_specs=[pl.BlockSpec((tm, d), lambda i, j: (i, 0)),
                  pl.BlockSpec((1, d), lambda i, j: (0, 0)),
                  pl.BlockSpec((None, None, 1, d), mod_map(0)),
                  pl.BlockSpec((None, None, 1, d), mod_map(1)),
                  pl.BlockSpec((None, tn, d), lambda i, j: (layer, j, 0)),
                  pl.BlockSpec((None, n_if, d), lambda i, j: (layer, MAIN_W // n_if, 0))],
        out_specs=(pl.BlockSpec((tm, tn), lambda i, j: (i, j)),
                   pl.BlockSpec((tm, 128), lambda i, j: (i, 0))),
        scratch_shapes=[pltpu.VMEM((tm, d), BF16)],
        compiler_params=_params("parallel", "arbitrary"),
        name="in_proj",
    )(x, norm_w, mod4, mod4, w_in_t, w_in_t)


def _outproj_kernel(a_ref, b_ref, c_ref, d_ref, w_ref, x_ref, g_ref, *rest, final):
    o_ref = rest[-1]
    for n in range(w_ref.shape[1] // GROUP_W):
        cols = slice(n * GROUP_W, (n + 1) * GROUP_W)
        acc = _dot(a_ref[...], w_ref[0:GROUP_W, cols])
        acc += _dot(b_ref[...], w_ref[GROUP_W:2 * GROUP_W, cols])
        acc += _dot(c_ref[...], w_ref[2 * GROUP_W:3 * GROUP_W, cols])
        acc += _dot(d_ref[...], w_ref[3 * GROUP_W:4 * GROUP_W, cols])
        o_ref[:, cols] = x_ref[:, cols] + g_ref[:, cols] * acc
    if final:
        fnw_ref = rest[0]
        for r in range(o_ref.shape[0] // ROW_CHUNK):
            rows = slice(r * ROW_CHUNK, (r + 1) * ROW_CHUNK)
            xn = o_ref[rows, :]
            o_ref[rows, :] = xn * lax.rsqrt(jnp.mean(xn * xn, axis=-1, keepdims=True) + NORM_EPS) * fnw_ref[...]


def _out_projection(branches, w_out, x, mod4, layer, mod_row0, rows_per_mod, final_norm_w):
    m, d = x.shape
    tm = 512
    per_batch = rows_per_mod < m
    if per_batch:
        gate_map = lambda i: (layer, mod_row0 + (i * tm) // rows_per_mod, 0, 2)
    else:
        gate_map = lambda i: (layer, mod_row0, 0, 2)
    final = final_norm_w is not None
    in_specs = [pl.BlockSpec((tm, GROUP_W), lambda i: (i, 0)) for _ in range(4)]
    in_specs += [pl.BlockSpec((d, d), lambda i: (0, 0)),
                 pl.BlockSpec((tm, d), lambda i: (i, 0)),
                 pl.BlockSpec((None, None, 1, d), gate_map)]
    args = list(branches) + [w_out, x, mod4]
    if final:
        in_specs.append(pl.BlockSpec((1, d), lambda i: (0, 0)))
        args.append(final_norm_w)
    return pl.pallas_call(
        functools.partial(_outproj_kernel, final=final),
        out_shape=jax.ShapeDtypeStruct((m, d), F32),
        grid=(m // tm,),
        in_specs=in_specs,
        out_specs=pl.BlockSpec((tm, d), lambda i: (i, 0)),
        compiler_params=_params("parallel"),
        name="out_proj",
    )(*args)


def _scan8(a, b, row, reverse):
    for dist in (1, 2, 4):
        if reverse:
            keep = row < 8 - dist
            shift = 8 - dist
        else:
            keep = row >= dist
            shift = dist
        a_sh = jnp.where(keep, pltpu.roll(a, shift, 0), 1.0)
        b_sh = jnp.where(keep, pltpu.roll(b, shift, 0), 0.0)
        b = b + a * b_sh
        a = a * a_sh
    return a, b


def _lru_kernel(x_ref, g_ref, cw_ref, cb_ref, wg_ref, br_ref, bi_ref, lam_ref, h0_ref,
                o_ref, st_ref, xc_s, pre_s, a_s, b_s, h_s, *, seq):
    w = GROUP_W
    rc = 256
    x = x_ref[...].astype(F32)
    t = lax.broadcasted_iota(jnp.int32, (seq, w), 0)
    xc = x * cw_ref[1:2, :] + cb_ref[...]
    xc += jnp.where(t >= 1, pltpu.roll(x, 1, 0), 0.0) * cw_ref[0:1, :]
    xc += jnp.where(t < seq - 1, pltpu.roll(x, seq - 1, 0), 0.0) * cw_ref[2:3, :]
    xc += jnp.where(t < seq - 2, pltpu.roll(x, seq - 2, 0), 0.0) * cw_ref[3:4, :]
    xc_s[...] = xc
    for g in range(HEADS):
        pre = _dot(xc_s[:, g * 128:(g + 1) * 128].astype(BF16), wg_ref[g])
        for k in range(4):
            pre_s[k, :, g * 128:(g + 1) * 128] = pre[:, k * 128:(k + 1) * 128]
    for d in range(2):
        log_lam = LRU_C * _log_sigmoid(lam_ref[d:d + 1, :])
        for c in range(seq // rc):
            rows = slice(c * rc, (c + 1) * rc)
            r = _sigmoid(pre_s[2 * d, rows, :] + br_ref[d:d + 1, :])
            i = _sigmoid(pre_s[2 * d + 1, rows, :] + bi_ref[d:d + 1, :])
            log_a = log_lam * r
            a = jnp.exp(log_a)
            a_s[d, rows, :] = a
            b_s[d, rows, :] = jnp.sqrt(-jnp.tanh(log_a) * (a * a + 1.0)) * i * xc_s[rows, :]

    row = lax.broadcasted_iota(jnp.int32, (8, w), 0)
    ngroups = seq // 8

    def body(g, carry):
        hf, hb = carry
        rf = pl.multiple_of(g * 8, 8)
        rb = pl.multiple_of((ngroups - 1 - g) * 8, 8)
        af, bf = _scan8(a_s[0, pl.ds(rf, 8), :], b_s[0, pl.ds(rf, 8), :], row, False)
        ab, bb = _scan8(a_s[1, pl.ds(rb, 8), :], b_s[1, pl.ds(rb, 8), :], row, True)
        hf_new = bf + af * hf
        hb_new = bb + ab * hb
        h_s[0, pl.ds(rf, 8), :] = hf_new
        h_s[1, pl.ds(rb, 8), :] = hb_new
        return (jnp.broadcast_to(hf_new[7:8, :], (8, w)), jnp.broadcast_to(hb_new[0:1, :], (8, w)))

    hf0 = jnp.broadcast_to(h0_ref[0:1, :], (8, w))
    hb0 = jnp.broadcast_to(h0_ref[1:2, :], (8, w))
    hf_last, hb_last = lax.fori_loop(0, ngroups, body, (hf0, hb0))
    st_ref[0:1, :] = hf_last[0:1, :]
    st_ref[1:2, :] = hb_last[0:1, :]
    for c in range(seq // rc):
        rows = slice(c * rc, (c + 1) * rc)
        o_ref[rows, :] = ((h_s[0, rows, :] + h_s[1, rows, :]) * _silu(g_ref[rows, :].astype(F32))).astype(BF16)


def _lru(p_main, batch, seq, conv_w, conv_b, w_gates, b_r, b_i, lam, h0):
    w = GROUP_W
    const = lambda *shape: pl.BlockSpec(shape, lambda b: (0,) * len(shape))
    return pl.pallas_call(
        functools.partial(_lru_kernel, seq=seq),
        out_shape=(jax.ShapeDtypeStruct((batch * seq, w), BF16), jax.ShapeDtypeStruct((batch, 2, w), F32)),
        grid=(batch,),
        in_specs=[pl.BlockSpec((seq, w), lambda b: (b, COL_LRU_X)),
                  pl.BlockSpec((seq, w), lambda b: (b, COL_LRU_G)),
                  const(4, w), const(1, w), const(HEADS, 128, 4 * 128), const(2, w), const(2, w), const(2, w),
                  pl.BlockSpec((None, 2, w), lambda b: (b, 0, 0))],
        out_specs=(pl.BlockSpec((seq, w), lambda b: (b, 0)),
                   pl.BlockSpec((None, 2, w), lambda b: (b, 0, 0))),
        scratch_shapes=[pltpu.VMEM((seq, w), F32), pltpu.VMEM((4, seq, w), F32),
                        pltpu.VMEM((2, seq, w), F32), pltpu.VMEM((2, seq, w), F32), pltpu.VMEM((2, seq, w), F32)],
        compiler_params=_params("parallel"),
        name="rg_lru",
    )(p_main, p_main, conv_w, conv_b, w_gates, b_r, b_i, lam, h0)


def _head_rms(x, w):
    return x * lax.rsqrt(jnp.mean(x * x, axis=-1, keepdims=True) + NORM_EPS) * w


def _rope(x, cos, sin_lo, sin_hi):
    return x * cos + pltpu.roll(x, 96, 1) * sin_lo + pltpu.roll(x, 32, 1) * sin_hi


def _attn_kernel(*refs, n_kv, has_cache, bias_plan, qk_norm, rope, write_kv):
    has_bias = bias_plan is not None
    it = iter(refs)
    q_ref, k_ref, v_ref, g_ref = next(it), next(it), next(it), next(it)
    kc_ref = vc_ref = bias_ref = qn_ref = kn_ref = None
    if has_cache:
        kc_ref, vc_ref = next(it), next(it)
    if has_bias:
        bias_ref = next(it)
    if qk_norm:
        qn_ref, kn_ref = next(it), next(it)
    if rope:
        cq_ref, slq_ref, shq_ref, ck_ref, slk_ref, shk_ref = (next(it) for _ in range(6))
    if write_kv:
        next(it), next(it)
    o_ref = next(it)
    if write_kv:
        ko_ref, vo_ref = next(it), next(it)
    prep_k = qk_norm or rope
    if prep_k:
        kb_s = next(it)
    vb_s = next(it)
    if has_cache:
        kcb_s, vcb_s = next(it), next(it)
    if has_bias:
        bias_s = next(it)
        @pl.when(jnp.logical_and(pl.program_id(0) == 0, pl.program_id(1) == 0))
        def _():
            for t, kind in enumerate(bias_plan.kinds):
                for h in range(HEADS):
                    for a, row in enumerate(kind):
                        for p, entry in enumerate(row):
                            bias_s[t, h, a * GRID_W:(a + 1) * GRID_W, p * 128:(p + 1) * 128] = bias_ref[h, entry]

    @pl.when(pl.program_id(1) == 0)
    def _():
        for kvh in range(n_kv):
            cols = slice(kvh * HEAD_DIM, (kvh + 1) * HEAD_DIM)
            vcols = slice(2 * kvh * HEAD_DIM, (2 * kvh + 1) * HEAD_DIM)
            ocols = slice((2 * kvh + 1) * HEAD_DIM, (2 * kvh + 2) * HEAD_DIM)
            k = k_ref[:, cols]
            v = v_ref[:, cols]
            if qk_norm:
                k = _head_rms(k.astype(F32), kn_ref[...])
            if write_kv:
                ko_ref[pl.ds(kvh, k.shape[0], stride=n_kv), :] = k.astype(F32)
                vo_ref[pl.ds(kvh, v.shape[0], stride=n_kv), :] = v.astype(F32)
            if rope:
                k = _rope(k.astype(F32), ck_ref[...], slk_ref[...], shk_ref[...])
            if prep_k:
                kb_s[:, cols] = k.astype(BF16)
            vb_s[:, vcols] = v.astype(BF16)
            vb_s[:, ocols] = jnp.ones((vb_s.shape[0], HEAD_DIM), BF16)
            if has_cache:
                lc = kcb_s.shape[0]
                kcb_s[:, cols] = kc_ref[pl.ds(kvh, lc, stride=n_kv), :].astype(BF16)
                vcb_s[:, vcols] = vc_ref[pl.ds(kvh, lc, stride=n_kv), :].astype(BF16)
                vcb_s[:, ocols] = jnp.ones((vcb_s.shape[0], HEAD_DIM), BF16)

    c2 = HEAD_DIM ** -0.5 * LOG2E
    group = HEADS // n_kv
    krows = slice(None)
    if has_bias:
        start, kind_id = 0, 0
        for blk in range(len(bias_plan.kind_of)):
            start = jnp.where(pl.program_id(1) == blk, bias_plan.key_start[blk], start)
            kind_id = jnp.where(pl.program_id(1) == blk, bias_plan.kind_of[blk], kind_id)
        krows = pl.ds(pl.multiple_of(start, 256), bias_plan.window)
    for kvh in range(n_kv):
        cols = slice(kvh * HEAD_DIM, (kvh + 1) * HEAD_DIM)
        v1cols = slice(2 * kvh * HEAD_DIM, (2 * kvh + 2) * HEAD_DIM)
        kb = kb_s[krows, cols] if prep_k else k_ref[krows, cols]
        v1 = vb_s[krows, v1cols]
        for gi in range(group):
            h = kvh * group + gi
            hc = slice(h * HEAD_DIM, (h + 1) * HEAD_DIM)
            q = q_ref[:, hc]
            if qk_norm:
                q = _head_rms(q.astype(F32), qn_ref[...])
            if rope:
                q = _rope(q.astype(F32), cq_ref[...], slq_ref[...], shq_ref[...])
            qb = q.astype(BF16)
            s1 = _dot_nt(qb, kb)
            if has_cache:
                s2 = _dot_nt(qb, kcb_s[:, cols])
            if has_bias:
                s1 = s1 * c2 + bias_s[kind_id, h]
                mx = jnp.max(s1, axis=-1, keepdims=True)
                if has_cache:
                    s2 = s2 * c2
                    mx = jnp.maximum(mx, jnp.max(s2, axis=-1, keepdims=True))
                p1 = jnp.exp2(s1 - mx)
                p2 = jnp.exp2(s2 - mx) if has_cache else None
            else:
                mx = jnp.max(s1, axis=-1, keepdims=True)
                if has_cache:
                    mx = jnp.maximum(mx, jnp.max(s2, axis=-1, keepdims=True))
                p1 = jnp.exp2((s1 - mx) * c2)
                p2 = jnp.exp2((s2 - mx) * c2) if has_cache else None
            acc = _dot(p1.astype(BF16), v1)
            if has_cache:
                acc = acc + _dot(p2.astype(BF16), vcb_s[:, v1cols])
            o = acc[:, 0:HEAD_DIM] / acc[:, HEAD_DIM:HEAD_DIM + 1]
            o_ref[:, hc] = (o * _silu(g_ref[:, hc].astype(F32))).astype(BF16)


def _attention(p_main, batch, seq, *, col_q, col_g, k_spec, v_spec, n_kv, cache=None, bias=None,
               qk_norm=None, rope=None, kv_out=None):
    tq = 256
    nq = seq // tq
    kvw = n_kv * HEAD_DIM
    in_specs = [pl.BlockSpec((tq, GROUP_W), lambda b, i: (b * nq + i, col_q)),
                pl.BlockSpec((seq, kvw), lambda b, i: (b, k_spec)),
                pl.BlockSpec((seq, kvw), lambda b, i: (b, v_spec)),
                pl.BlockSpec((tq, GROUP_W), lambda b, i: (b * nq + i, col_g))]
    args = [p_main, p_main, p_main, p_main]
    scratch = [pltpu.VMEM((seq, kvw), BF16)] if (qk_norm is not None or rope is not None) else []
    scratch.append(pltpu.VMEM((seq, 2 * kvw), BF16))
    if cache is not None:
        kc, vc, layer = cache
        lc = kc.shape[2] // n_kv
        in_specs += [pl.BlockSpec((None, None, lc * n_kv, HEAD_DIM), lambda b, i: (b, layer, 0, 0))] * 2
        args += [kc, vc]
        scratch += [pltpu.VMEM((lc, kvw), BF16), pltpu.VMEM((lc, 2 * kvw), BF16)]
    bias_plan = None
    if bias is not None:
        pairs, layer_b, bias_plan = bias
        in_specs.append(pl.BlockSpec((None,) + pairs.shape[1:], lambda b, i: (layer_b, 0, 0, 0, 0)))
        args.append(pairs)
        scratch.append(pltpu.VMEM((len(bias_plan.kinds), HEADS, tq, bias_plan.window), F32))
    if qk_norm is not None:
        in_specs += [pl.BlockSpec((1, HEAD_DIM), lambda b, i: (0, 0))] * 2
        args += list(qk_norm)
    if rope is not None:
        in_specs += [pl.BlockSpec((tq, HEAD_DIM), lambda b, i: (i, 0))] * 3
        in_specs += [pl.BlockSpec((seq, HEAD_DIM), lambda b, i: (0, 0))] * 3
        args += list(rope) + list(rope)
    out_shape = [jax.ShapeDtypeStruct((batch * seq, GROUP_W), BF16)]
    out_specs = [pl.BlockSpec((tq, GROUP_W), lambda b, i: (b * nq + i, 0))]
    aliases = {}
    if kv_out is not None:
        assert nq == 1
        k_all, v_all, layer_o = kv_out
        for j, arr in enumerate((k_all, v_all)):
            aliases[len(args)] = 1 + j
            in_specs.append(pl.BlockSpec(memory_space=pl.ANY))
            args.append(arr)
            out_shape.append(jax.ShapeDtypeStruct(arr.shape, arr.dtype))
            out_specs.append(pl.BlockSpec((None, None, seq * n_kv, HEAD_DIM), lambda b, i: (b, layer_o, 0, 0)))
    return pl.pallas_call(
        functools.partial(_attn_kernel, n_kv=n_kv, has_cache=cache is not None, bias_plan=bias_plan,
                          qk_norm=qk_norm is not None, rope=rope is not None, write_kv=kv_out is not None),
        out_shape=tuple(out_shape),
        grid=(batch, nq),
        in_specs=in_specs,
        out_specs=tuple(out_specs),
        scratch_shapes=scratch,
        input_output_aliases=aliases,
        compiler_params=_params("arbitrary", "arbitrary"),
        name="attention",
    )(*args)


def _mlstm_kernel(*refs, seq, hp, has_state, write_state):
    it = iter(refs)
    q_ref, k_ref, v_ref, og_ref, g_ref, gates_ref, gb_ref, onw_ref = (next(it) for _ in range(8))
    if has_state:
        c0_ref, n0_ref, m0_ref = next(it), next(it), next(it)
    if write_state:
        next(it)
    out_ref = next(it)
    if write_state:
        cout_ref, nout_ref, mout_ref = next(it), next(it), next(it)
    ht_s = next(it)

    L = ML_CHUNK
    nc = seq // L
    hd = HEAD_DIM
    scale = hd ** -0.5
    gt = (gates_ref[...] + gb_ref[...]).T[0:4 * HEADS, :]
    gate_id = lax.broadcasted_iota(jnp.int32, gt.shape, 0)
    gt = jnp.where((gate_id // HEADS) % 2 == 1, _log_sigmoid(gt), gt) * LOG2E

    def head_of(j):
        return j if hp == HEADS else pl.program_id(1) * hp + j

    def gate_row(r):
        if isinstance(r, int):
            return gt[r:r + 1, :]
        return jnp.sum(jnp.where(gate_id == r, gt, 0.0), axis=0, keepdims=True)

    si = lax.broadcasted_iota(jnp.int32, (L, L), 0)
    ti = lax.broadcasted_iota(jnp.int32, (L, L), 1)
    pick3 = jnp.where(lax.broadcasted_iota(jnp.int32, (16, hd), 0) < 3, 1.0, 0.0).astype(BF16)
    row16 = lax.broadcasted_iota(jnp.int32, (16, L), 0)

    def stack3(x):
        x1, x2, x3 = (p.astype(F32) for p in _split3(x))
        stacked = jnp.where(row16 == 0, x1, jnp.where(row16 == 1, x2, jnp.where(row16 == 2, x3, 0.0)))
        return stacked.astype(BF16)

    masks = [si <= ti, si >= ti]
    tris = [jnp.where(mk, 1.0, 0.0).astype(BF16) for mk in masks]
    edges = [L - 1, 0]
    units = [(j, d) for j in range(hp) for d in range(2)]
    ii_row, ff_row, state = {}, {}, {}
    for j, d in units:
        h = head_of(j)
        ii_row[j, d] = gate_row(d * 8 + h)
        ff_row[j, d] = gate_row(d * 8 + HEADS + h)
        if has_state:
            state[j, d] = (c0_ref[d, j].T, n0_ref[d, pl.ds(h, 1), :], m0_ref[d, pl.ds(h, 1), 0:1] * LOG2E)
        else:
            state[j, d] = (jnp.zeros((hd, hd), F32), jnp.zeros((1, hd), F32), jnp.zeros((1, 1), F32))

    ones_cols = jnp.ones((L, hd), BF16)

    def load_chunk(j, c):
        hc = slice(j * hd, (j + 1) * hd)
        rws = slice(c * L, (c + 1) * L)
        k = k_ref[rws, hc].astype(F32)
        return ((q_ref[rws, hc].astype(F32) * scale).astype(BF16), k, k.astype(BF16),
                jnp.concatenate([v_ref[rws, hc].astype(BF16), ones_cols], axis=1))

    loaded = {(j, c): load_chunk(j, c) for j in range(hp) for c in range(nc)}
    for step in range(nc):
        chunk = [step, nc - 1 - step]
        rows = {u: slice(chunk[u[1]] * L, (chunk[u[1]] + 1) * L) for u in units}
        b, g, g_col, g_src, mx = {}, {}, {}, {}, {}
        for u in units:
            b3 = _dot(stack3(ff_row[u][:, rows[u]]), tris[u[1]])
            b[u] = b3[0:1, :] + b3[1:2, :] + b3[2:3, :]
            g[u] = ii_row[u][:, rows[u]] - b[u]
        for u in units:
            g_col[u] = _dot_tn(stack3(g[u]), pick3)
        for u in units:
            g_src[u] = jnp.where(masks[u[1]], jnp.concatenate([g_col[u]] * (L // hd), axis=1), NEG_INF)
            mx[u] = jnp.maximum(state[u][2], jnp.max(g_src[u], axis=0, keepdims=True))
        for u in units:
            j, d = u
            ct, n, m = state[u]
            qb, k, kb, v1 = loaded[j, chunk[d]]
            p = _dot_nt(kb, qb) * jnp.exp2(g_src[u] - mx[u])
            w_inter = jnp.exp2(m - mx[u])
            n16 = jnp.broadcast_to(n, (16, hd)).astype(BF16)
            den = jnp.sum(p, axis=0, keepdims=True) + w_inter * _dot_nt(n16, qb)[0:1, :]
            num = _dot_tn(v1[:, 0:hd], p.astype(BF16)) + w_inter * _dot_nt(ct.astype(BF16), qb)
            ht_s[d, j, :, rows[u]] = num / jnp.maximum(jnp.abs(den), jnp.exp2(-(b[u] + mx[u])))
            mx_end = mx[u][:, edges[d]:edges[d] + 1]
            w_prev = jnp.exp2(m - mx_end)
            ku = k * jnp.exp2(g_col[u] - mx_end)
            kv = _dot_tn(v1, ku.astype(BF16))
            state[u] = (w_prev * ct + kv[0:hd, :], w_prev * n + kv[hd:hd + 1, :],
                        b[u][:, edges[d]:edges[d] + 1] + mx_end)

    if write_state:
        for j, d in units:
            ct, n, m = state[j, d]
            cout_ref[d, j] = ct.T
            nout_ref[d, pl.ds(head_of(j), 1), :] = n
            mout_ref[d, pl.ds(head_of(j), 1), :] = jnp.broadcast_to(m * LN2, (1, hd))

    for j in range(hp):
        hc = slice(j * hd, (j + 1) * hd)
        hs = (ht_s[0, j] + ht_s[1, j]).T
        y = hs * lax.rsqrt(jnp.mean(hs * hs, axis=-1, keepdims=True) + NORM_EPS) * onw_ref[:, hc]
        out_ref[:, hc] = (y * _sigmoid(og_ref[:, hc].astype(F32)) * _silu(g_ref[:, hc].astype(F32))).astype(BF16)


def _mlstm(p_main, p_if, batch, seq, gate_b, out_norm_w, hp, state=None, state_out=None):
    hd = HEAD_DIM
    nh = HEADS // hp
    w = hp * hd
    col = lambda base: (lambda b, h: (b, base * nh + h))
    in_specs = [pl.BlockSpec((seq, w), col(COL_ML_Q)), pl.BlockSpec((seq, w), col(COL_ML_K)),
                pl.BlockSpec((seq, w), col(COL_ML_V)), pl.BlockSpec((seq, w), col(COL_ML_O)),
                pl.BlockSpec((seq, w), col(COL_ML_G)),
                pl.BlockSpec((seq, 128), lambda b, h: (b, 0)),
                pl.BlockSpec((1, 128), lambda b, h: (0, 0)),
                pl.BlockSpec((1, w), lambda b, h: (0, h))]
    args = [p_main] * 5 + [p_if, jnp.pad(gate_b, (0, 128 - gate_b.shape[0])).reshape(1, 128),
                           out_norm_w.reshape(1, GROUP_W)]
    if state is not None:
        c0, n0, m0, layer = state
        m0b = jnp.broadcast_to(m0[..., None], m0.shape + (hd,))
        in_specs += [pl.BlockSpec((None, None, 2, hp, hd, hd), lambda b, h: (b, layer, 0, h, 0, 0)),
                     pl.BlockSpec((None, None, 2, HEADS, hd), lambda b, h: (b, layer, 0, 0, 0)),
                     pl.BlockSpec((None, None, 2, HEADS, hd), lambda b, h: (b, layer, 0, 0, 0))]
        args += [c0, n0, m0b]
    out_shape = [jax.ShapeDtypeStruct((batch * seq, GROUP_W), BF16)]
    out_specs = [pl.BlockSpec((seq, w), lambda b, h: (b, h))]
    aliases = {}
    if state_out is not None:
        c_all, layer_o = state_out
        aliases[len(args)] = 1
        in_specs.append(pl.BlockSpec(memory_space=pl.ANY))
        args.append(c_all)
        out_shape += [jax.ShapeDtypeStruct(c_all.shape, c_all.dtype),
                      jax.ShapeDtypeStruct((batch, 2, HEADS, hd), F32),
                      jax.ShapeDtypeStruct((batch, 2, HEADS, hd), F32)]
        out_specs += [pl.BlockSpec((None, None, 2, hp, hd, hd), lambda b, h: (b, layer_o, 0, h, 0, 0)),
                      pl.BlockSpec((None, 2, HEADS, hd), lambda b, h: (b, 0, 0, 0)),
                      pl.BlockSpec((None, 2, HEADS, hd), lambda b, h: (b, 0, 0, 0))]
    return pl.pallas_call(
        functools.partial(_mlstm_kernel, seq=seq, hp=hp, has_state=state is not None,
                          write_state=state_out is not None),
        out_shape=tuple(out_shape),
        grid=(batch, nh),
        in_specs=in_specs,
        out_specs=tuple(out_specs),
        scratch_shapes=[pltpu.VMEM((2, hp, hd, seq), F32)],
        input_output_aliases=aliases,
        compiler_params=_params("parallel", "arbitrary"),
        name="mlstm",
    )(*args)


class BiasPlan(NamedTuple):
    kinds: tuple
    kind_of: tuple
    key_start: tuple
    window: int


NA_Q_ROWS = 4
NA_K_ROWS = 12


def _na_bias(rpb, seq):
    rows = seq // GRID_W
    wr = min(NA_ROWS, rows)
    assert rows % NA_Q_ROWS == 0 and rows >= NA_K_ROWS
    c = np.arange(GRID_W)
    c_start = np.clip(c - NA_COLS // 2, 0, GRID_W - NA_COLS)
    col_ok = (c[None, :] >= c_start[:, None]) & (c[None, :] < c_start[:, None] + NA_COLS)
    col_off = np.clip(c[None, :] - c[:, None] + NA_COLS - 1, 0, 2 * NA_COLS - 2)
    col_sel = (col_off[..., None] == np.arange(2 * NA_COLS - 1)).astype(np.float32)
    blocks = jnp.einsum('lhij,cdj->lhicd', rpb.astype(F32), col_sel, precision=lax.Precision.HIGHEST)
    blocks = jnp.where(col_ok, blocks * LOG2E, NEG_INF)
    blocks = jnp.concatenate([blocks, jnp.full_like(blocks[:, :, :1], NEG_INF)], axis=2)
    masked = 2 * NA_ROWS - 1
    nblk = rows // NA_Q_ROWS
    key_row0 = [int(min(np.clip(i * NA_Q_ROWS - wr // 2, 0, rows - wr), rows - NA_K_ROWS)) for i in range(nblk)]
    idx = np.full((nblk, NA_Q_ROWS, NA_K_ROWS), masked, np.int32)
    for i in range(nblk):
        for a in range(NA_Q_ROWS):
            qr = i * NA_Q_ROWS + a
            r_start = int(np.clip(qr - wr // 2, 0, rows - wr))
            assert key_row0[i] <= r_start and r_start + wr <= key_row0[i] + NA_K_ROWS
            for kk in range(NA_K_ROWS):
                kr = key_row0[i] + kk
                if r_start <= kr < r_start + wr:
                    idx[i, a, kk] = kr - qr + NA_ROWS - 1
    kinds, kind_of = np.unique(idx.reshape(nblk, -1), axis=0, return_inverse=True)
    pair_of = kinds.reshape(-1, NA_Q_ROWS, NA_K_ROWS // 2, 2)
    pairs, entry = np.unique(pair_of.reshape(-1, 2), axis=0, return_inverse=True)
    table = jnp.concatenate([blocks[:, :, pairs[:, 0]], blocks[:, :, pairs[:, 1]]], axis=-1)
    entry = np.ravel(entry).reshape(pair_of.shape[:3])
    plan = BiasPlan(kinds=tuple(tuple(tuple(int(e) for e in row) for row in kind) for kind in entry),
                    kind_of=tuple(int(t) for t in np.ravel(kind_of)),
                    key_start=tuple(r * GRID_W for r in key_row0), window=NA_K_ROWS * GRID_W)
    return table, plan


def _rope_tables(seq):
    t = jnp.arange(seq)
    nf = HEAD_DIM // 4
    inv = ROPE_THETA ** (-jnp.arange(nf, dtype=F32) / nf)
    zeros = jnp.zeros((seq, nf), F32)

    def half(pos):
        ang = pos.astype(F32)[:, None] * inv
        cs, sn = jnp.cos(ang), jnp.sin(ang)
        return jnp.concatenate([cs, cs], -1), jnp.concatenate([-sn, zeros], -1), jnp.concatenate([zeros, sn], -1)

    parts = [half(t // GRID_W), half(t % GRID_W)]
    return tuple(jnp.concatenate([parts[0][i], parts[1][i]], -1) for i in range(3))


def kernel(x_prompt, x_sample, cache_na_k, cache_na_v, cache_gqa_k, cache_gqa_v, state_lru, state_mlstm_C,
           state_mlstm_n, state_mlstm_m, c, c_ctx, norm_w, ada_w, ada_b, w_in, lru_conv_w, lru_conv_b, lru_wr,
           lru_br, lru_wi, lru_bi, lru_lambda, na_rpb, gqa_qnorm, gqa_knorm, ml_gate_b, ml_out_norm, w_out,
           final_norm_w):
    bp, tp, d = x_prompt.shape
    bs, ts, _ = x_sample.shape
    depth = w_in.shape[0]
    past = cache_na_k.shape[2]

    cond = jnp.zeros((COND_ROWS, d), F32).at[0].set(c_ctx).at[1:1 + bs].set(c)
    mod4 = _modulation(cond, ada_w, ada_b).reshape(depth, COND_ROWS, 1, 3 * d)

    w_in_t = jnp.swapaxes(w_in, 1, 2)
    w_out_b = w_out.astype(BF16)
    w_gates = jnp.concatenate([lru_wr[:, 0], lru_wi[:, 0], lru_wr[:, 1], lru_wi[:, 1]], axis=-1).astype(BF16)
    bias_pairs, bias_plan = _na_bias(na_rpb, ts)
    rope = _rope_tables(ts)
    zeros_h0 = jnp.zeros((bp, 2, GROUP_W), F32)

    xp = x_prompt.reshape(bp * tp, d)
    xs = x_sample.reshape(bs * ts, d)
    norm_w2 = norm_w.reshape(depth, 1, d)
    fnw = final_norm_w.reshape(1, d)
    na_k = jnp.zeros((bp, depth, tp * HEADS, HEAD_DIM), F32)
    na_v = jnp.zeros((bp, depth, tp * HEADS, HEAD_DIM), F32)
    gqa_k = jnp.zeros((bp, depth, tp * GQA_KV_HEADS, HEAD_DIM), F32)
    gqa_v = jnp.zeros((bp, depth, tp * GQA_KV_HEADS, HEAD_DIM), F32)
    st_c = jnp.zeros((bp, depth, 2, HEADS, HEAD_DIM, HEAD_DIM), F32)
    flat = lambda a: a.reshape(a.shape[0], a.shape[1], a.shape[2] * a.shape[3], a.shape[4])
    cache_na = (flat(cache_na_k), flat(cache_na_v))
    cache_gqa = (flat(cache_gqa_k), flat(cache_gqa_v))
    new = {k: [] for k in ('lru', 'n', 'm')}

    for l in range(depth):
        last = l == depth - 1
        lru_args = (lru_conv_w[l], lru_conv_b[l].reshape(1, GROUP_W), w_gates[l], lru_br[l], lru_bi[l], lru_lambda[l])
        qk_norm = (gqa_qnorm[l].reshape(1, HEAD_DIM), gqa_knorm[l].reshape(1, HEAD_DIM))

        p, pif = _in_projection(xp, norm_w2[l], mod4, l, 0, bp * tp, w_in_t)
        out_a, st_lru = _lru(p, bp, tp, *lru_args, zeros_h0)
        out_b, na_k, na_v = _attention(p, bp, tp, col_q=COL_NA_Q, col_g=COL_NA_G, k_spec=COL_NA_K,
                                       v_spec=COL_NA_V, n_kv=HEADS, kv_out=(na_k, na_v, l))
        out_c, gqa_k, gqa_v = _attention(p, bp, tp, col_q=COL_GQA_Q, col_g=COL_GQA_G, k_spec=2 * COL_GQA_KV,
                                         v_spec=2 * COL_GQA_KV + 1, n_kv=GQA_KV_HEADS, qk_norm=qk_norm,
                                         kv_out=(gqa_k, gqa_v, l))
        out_d, st_c, st_n, st_m = _mlstm(p, pif, bp, tp, ml_gate_b[l], ml_out_norm[l], HEADS, state_out=(st_c, l))
        xp = _out_projection((out_a, out_b, out_c, out_d), w_out_b[l], xp, mod4, l, 0, bp * tp,
                             fnw if last else None)
        new['lru'].append(st_lru)
        new['n'].append(st_n)
        new['m'].append(st_m[..., 0])

        p, pif = _in_projection(xs, norm_w2[l], mod4, l, 1, ts, w_in_t)
        out_a, _ = _lru(p, bs, ts, *lru_args, state_lru[:, l])
        (out_b,) = _attention(p, bs, ts, col_q=COL_NA_Q, col_g=COL_NA_G, k_spec=COL_NA_K, v_spec=COL_NA_V,
                              n_kv=HEADS, cache=cache_na + (l,), bias=(bias_pairs, l, bias_plan))
        (out_c,) = _attention(p, bs, ts, col_q=COL_GQA_Q, col_g=COL_GQA_G, k_spec=2 * COL_GQA_KV,
                              v_spec=2 * COL_GQA_KV + 1, n_kv=GQA_KV_HEADS, cache=cache_gqa + (l,),
                              qk_norm=qk_norm, rope=rope)
        (out_d,) = _mlstm(p, pif, bs, ts, ml_gate_b[l], ml_out_norm[l], HEADS,
                          state=(state_mlstm_C, state_mlstm_n, state_mlstm_m, l))
        xs = _out_projection((out_a, out_b, out_c, out_d), w_out_b[l], xs, mod4, l, 1, ts,
                             fnw if last else None)

    stacked = {k: jnp.stack(v, axis=1) for k, v in new.items()}
    heads = lambda a, n: a.reshape(bp, depth, tp, n, HEAD_DIM)
    return (xp.reshape(bp, tp, d), xs.reshape(bs, ts, d), heads(na_k, HEADS), heads(na_v, HEADS),
            heads(gqa_k, GQA_KV_HEADS), heads(gqa_v, GQA_KV_HEADS),
            stacked['lru'], st_c, stacked['n'], stacked['m'])
```

```python
import functools
from typing import NamedTuple

import numpy as np
import jax
import jax.numpy as jnp
from jax import lax
from jax.experimental import pallas as pl
from jax.experimental.pallas import tpu as pltpu

F32 = jnp.float32
BF16 = jnp.bfloat16

HEAD_DIM = 128
GROUP_W = 512
HEADS = GROUP_W // HEAD_DIM
GQA_KV_HEADS = 2
GRID_W = 64
NORM_EPS = 1e-6
NEG_INF = -1e30
LOG2E = 1.4426950408889634
LN2 = 0.6931471805599453
LRU_C = 8.0
NA_ROWS, NA_COLS = 8, 16
ROPE_THETA = 10000.0
ML_CHUNK = 256
COND_ROWS = 16
ROW_CHUNK = 256
SCAN_BLOCK = 64
W_CAST_CHUNKS = 8
VMEM_LIMIT = 52 * 1024 * 1024

COL_LRU_X, COL_LRU_G, COL_NA_Q, COL_NA_K, COL_NA_V, COL_NA_G = 0, 1, 2, 3, 4, 5
COL_GQA_Q, COL_GQA_KV, COL_GQA_G = 6, 7, 8
COL_ML_Q, COL_ML_K, COL_ML_V, COL_ML_O, COL_ML_G = 9, 10, 11, 12, 13
MAIN_W = 14 * GROUP_W


def _params(*sem):
    return pltpu.CompilerParams(dimension_semantics=sem, vmem_limit_bytes=VMEM_LIMIT)


def _sigmoid(x):
    return 1.0 / (1.0 + jnp.exp(-x))


def _silu(x):
    return x * _sigmoid(x)


def _log_sigmoid(x):
    return jnp.minimum(x, 0.0) - jnp.log1p(jnp.exp(-jnp.abs(x)))


def _dot(a, b):
    return jnp.dot(a, b, preferred_element_type=F32)


def _dot_nt(a, b):
    return lax.dot_general(a, b, (((1,), (1,)), ((), ())), preferred_element_type=F32)


def _dot_tn(a, b):
    return lax.dot_general(a, b, (((0,), (0,)), ((), ())), preferred_element_type=F32)


def _split3(x):
    x1 = x.astype(BF16)
    r = x - x1.astype(F32)
    x2 = r.astype(BF16)
    x3 = (r - x2.astype(F32)).astype(BF16)
    return x1, x2, x3


def _mod_kernel(cond_ref, w_ref, b_ref, o_ref):
    s = _silu(cond_ref[...]).astype(BF16)
    o_ref[...] = _dot(s, w_ref[...].astype(BF16)) + b_ref[...]


def _modulation(cond, ada_w, ada_b):
    depth, d, n = ada_w.shape
    tn = 1024
    return pl.pallas_call(
        _mod_kernel,
        out_shape=jax.ShapeDtypeStruct((depth, COND_ROWS, n), F32),
        grid=(depth, n // tn),
        in_specs=[pl.BlockSpec((COND_ROWS, d), lambda l, j: (0, 0)),
                  pl.BlockSpec((None, d, tn), lambda l, j: (l, 0, j)),
                  pl.BlockSpec((None, 1, tn), lambda l, j: (l, 0, j))],
        out_specs=pl.BlockSpec((None, COND_ROWS, tn), lambda l, j: (l, 0, j)),
        compiler_params=_params("parallel", "parallel"),
        name="adaln_mod",
    )(cond, ada_w, ada_b.reshape(depth, 1, n))


def _inproj_kernel(x_ref, nw_ref, sh_ref, sc_ref, w_ref, wif_ref, o_ref, oif_ref, xm_ref, *, n_if):
    @pl.when(pl.program_id(1) == 0)
    def _():
        gain = nw_ref[...] * (1.0 + sc_ref[...])
        for r in range(x_ref.shape[0] // ROW_CHUNK):
            rows = slice(r * ROW_CHUNK, (r + 1) * ROW_CHUNK)
            x = x_ref[rows, :]
            inv = lax.rsqrt(jnp.mean(x * x, axis=-1, keepdims=True) + NORM_EPS)
            xm_ref[rows, :] = (x * inv * gain + sh_ref[...]).astype(BF16)
        wif = jnp.concatenate([wif_ref[...], jnp.zeros((128 - n_if, wif_ref.shape[1]), F32)], axis=0)
        oif_ref[...] = _dot_nt(xm_ref[...], wif.astype(BF16))

    kc = w_ref.shape[1] // W_CAST_CHUNKS
    acc = _dot_nt(xm_ref[:, 0:kc], w_ref[:, 0:kc].astype(BF16))
    for c in range(1, W_CAST_CHUNKS):
        acc += _dot_nt(xm_ref[:, c * kc:(c + 1) * kc], w_ref[:, c * kc:(c + 1) * kc].astype(BF16))
    o_ref[...] = acc.astype(o_ref.dtype)


def _in_projection(x, norm_w, mod4, layer, mod_row0, rows_per_mod, w_in_t):
    m, d = x.shape
    tm, tn = min(1024, m), 1024
    n_if = w_in_t.shape[1] - MAIN_W
    assert m % tm == 0 and rows_per_mod % tm == 0 and n_if % 8 == 0 and MAIN_W % n_if == 0 and n_if <= 128
    per_batch = rows_per_mod < m

    def mod_map(col):
        if per_batch:
            return lambda i, j: (layer, mod_row0 + (i * tm) // rows_per_mod, 0, col)
        return lambda i, j: (layer, mod_row0, 0, col)

    return pl.pallas_call(
        functools.partial(_inproj_kernel, n_if=n_if),
        out_shape=(jax.ShapeDtypeStruct((m, MAIN_W), BF16), jax.ShapeDtypeStruct((m, 128), F32)),
        grid=(m // tm, MAIN_W // tn),
        in_specs=[pl.BlockSpec((tm, d), lambda i, j: (i, 0)),
                  pl.BlockSpec((1, d), lambda i, j: (0, 0)),
                  pl.BlockSpec((None, None, 1, d), mod_map(0)),
                  pl.BlockSpec((None, None, 1, d), mod_map(1)),
                  pl.BlockSpec((None, tn, d), lambda i, j: (layer, j, 0)),
                  pl.BlockSpec((None, n_if, d), lambda i, j: (layer, MAIN_W // n_if, 0))],
        out_specs=(pl.BlockSpec((tm, tn), lambda i, j: (i, j)),
                   pl.BlockSpec((tm, 128), lambda i, j: (i, 0))),
        scratch_shapes=[pltpu.VMEM((tm, d), BF16)],
        compiler_params=_params("parallel", "arbitrary"),
        name="in_proj",
    )(x, norm_w, mod4, mod4, w_in_t, w_in_t)


def _outproj_kernel(a_ref, b_ref, c_ref, d_ref, w_ref, x_ref, g_ref, *rest, final):
    o_ref = rest[-1]
    for n in range(w_ref.shape[1] // GROUP_W):
        cols = slice(n * GROUP_W, (n + 1) * GROUP_W)
        acc = _dot(a_ref[...], w_ref[0:GROUP_W, cols])
        acc += _dot(b_ref[...], w_ref[GROUP_W:2 * GROUP_W, cols])
        acc += _dot(c_ref[...], w_ref[2 * GROUP_W:3 * GROUP_W, cols])
        acc += _dot(d_ref[...], w_ref[3 * GROUP_W:4 * GROUP_W, cols])
        o_ref[:, cols] = x_ref[:, cols] + g_ref[:, cols] * acc
    if final:
        fnw_ref = rest[0]
        for r in range(o_ref.shape[0] // ROW_CHUNK):
            rows = slice(r * ROW_CHUNK, (r + 1) * ROW_CHUNK)
            xn = o_ref[rows, :]
            o_ref[rows, :] = xn * lax.rsqrt(jnp.mean(xn * xn, axis=-1, keepdims=True) + NORM_EPS) * fnw_ref[...]


def _out_projection(branches, w_out, x, mod4, layer, mod_row0, rows_per_mod, final_norm_w):
    m, d = x.shape
    tm = 512
    per_batch = rows_per_mod < m
    if per_batch:
        gate_map = lambda i: (layer, mod_row0 + (i * tm) // rows_per_mod, 0, 2)
    else:
        gate_map = lambda i: (layer, mod_row0, 0, 2)
    final = final_norm_w is not None
    in_specs = [pl.BlockSpec((tm, GROUP_W), lambda i: (i, 0)) for _ in range(4)]
    in_specs += [pl.BlockSpec((d, d), lambda i: (0, 0)),
                 pl.BlockSpec((tm, d), lambda i: (i, 0)),
                 pl.BlockSpec((None, None, 1, d), gate_map)]
    args = list(branches) + [w_out, x, mod4]
    if final:
        in_specs.append(pl.BlockSpec((1, d), lambda i: (0, 0)))
        args.append(final_norm_w)
    return pl.pallas_call(
        functools.partial(_outproj_kernel, final=final),
        out_shape=jax.ShapeDtypeStruct((m, d), F32),
        grid=(m // tm,),
        in_specs=in_specs,
        out_specs=pl.BlockSpec((tm, d), lambda i: (i, 0)),
        compiler_params=_params("parallel"),
        name="out_proj",
    )(*args)


def _scan8(a, b, row, reverse):
    for dist in (1, 2, 4):
        if reverse:
            keep = row < 8 - dist
            shift = 8 - dist
        else:
            keep = row >= dist
            shift = dist
        a_sh = jnp.where(keep, pltpu.roll(a, shift, 0), 1.0)
        b_sh = jnp.where(keep, pltpu.roll(b, shift, 0), 0.0)
        b = b + a * b_sh
        a = a * a_sh
    return a, b


def _lru_kernel(x_ref, g_ref, cw_ref, cb_ref, wg_ref, br_ref, bi_ref, lam_ref, h0_ref,
                o_ref, st_ref, xc_s, pre_s, a_s, b_s, h_s, *, seq):
    w = GROUP_W
    nlb = w // 128
    rc = 256
    x = x_ref[...].astype(F32)
    t = lax.broadcasted_iota(jnp.int32, (seq, w), 0)
    xc = x * cw_ref[1:2, :] + cb_ref[...]
    xc += jnp.where(t >= 1, pltpu.roll(x, 1, 0), 0.0) * cw_ref[0:1, :]
    xc += jnp.where(t < seq - 1, pltpu.roll(x, seq - 1, 0), 0.0) * cw_ref[2:3, :]
    xc += jnp.where(t < seq - 2, pltpu.roll(x, seq - 2, 0), 0.0) * cw_ref[3:4, :]
    xc_s[...] = xc
    for g in range(HEADS):
        pre = _dot(xc_s[:, g * 128:(g + 1) * 128].astype(BF16), wg_ref[g])
        for k in range(4):
            pre_s[k, :, g * 128:(g + 1) * 128] = pre[:, k * 128:(k + 1) * 128]
    for d in range(2):
        log_lam = LRU_C * _log_sigmoid(lam_ref[d:d + 1, :])
        for c in range(seq // rc):
            rows = slice(c * rc, (c + 1) * rc)
            r = _sigmoid(pre_s[2 * d, rows, :] + br_ref[d:d + 1, :])
            i = _sigmoid(pre_s[2 * d + 1, rows, :] + bi_ref[d:d + 1, :])
            log_a = log_lam * r
            a = jnp.exp(log_a)
            b = jnp.sqrt(-jnp.tanh(log_a) * (a * a + 1.0)) * i * xc_s[rows, :]
            for lb in range(nlb):
                a_s[d, lb, rows, :] = a[:, lb * 128:(lb + 1) * 128]
                b_s[d, lb, rows, :] = b[:, lb * 128:(lb + 1) * 128]

    row = lax.broadcasted_iota(jnp.int32, (8, 128), 0)
    nblocks = seq // SCAN_BLOCK

    def body(g, carry):
        units = [(d, lb) for d in range(2) for lb in range(nlb)]
        r0 = [pl.multiple_of(g * SCAN_BLOCK, SCAN_BLOCK), pl.multiple_of((nblocks - 1 - g) * SCAN_BLOCK, SCAN_BLOCK)]
        hs = {u: [None] * 8 for u in units}
        ps = {u: [None] * 8 for u in units}
        for step in range(8):
            for u in units:
                d, lb = u
                j, prev = (step, step - 1) if d == 0 else (7 - step, 8 - step)
                a = a_s[d, lb, pl.ds(r0[d] + j, 8, stride=8), :]
                b = b_s[d, lb, pl.ds(r0[d] + j, 8, stride=8), :]
                hs[u][j] = b if step == 0 else a * hs[u][prev] + b
                ps[u][j] = a if step == 0 else a * ps[u][prev]
        new = []
        for i, u in enumerate(units):
            d, lb = u
            tot = 7 if d == 0 else 0
            a_seg, b_seg = _scan8(ps[u][tot], hs[u][tot], row, d == 1)
            end = b_seg + a_seg * carry[i]
            if d == 0:
                enter = jnp.where(row >= 1, pltpu.roll(end, 1, 0), carry[i])
            else:
                enter = jnp.where(row < 7, pltpu.roll(end, 7, 0), carry[i])
            for j in range(8):
                h_s[d, lb, pl.ds(r0[d] + j, 8, stride=8), :] = hs[u][j] + ps[u][j] * enter
            new.append(jnp.broadcast_to(end[7:8, :] if d == 0 else end[0:1, :], (8, 128)))
        return tuple(new)

    init = tuple(jnp.broadcast_to(h0_ref[d:d + 1, lb * 128:(lb + 1) * 128], (8, 128))
                 for d in range(2) for lb in range(nlb))
    final = lax.fori_loop(0, nblocks, body, init)
    for d in range(2):
        for lb in range(nlb):
            st_ref[d:d + 1, lb * 128:(lb + 1) * 128] = final[d * nlb + lb][0:1, :]
    for c in range(seq // rc):
        rows = slice(c * rc, (c + 1) * rc)
        for lb in range(nlb):
            cols = slice(lb * 128, (lb + 1) * 128)
            h_sum = h_s[0, lb, rows, :] + h_s[1, lb, rows, :]
            o_ref[rows, cols] = (h_sum * _silu(g_ref[rows, cols].astype(F32))).astype(BF16)


def _lru(p_main, batch, seq, conv_w, conv_b, w_gates, b_r, b_i, lam, h0):
    w = GROUP_W
    const = lambda *shape: pl.BlockSpec(shape, lambda b: (0,) * len(shape))
    return pl.pallas_call(
        functools.partial(_lru_kernel, seq=seq),
        out_shape=(jax.ShapeDtypeStruct((batch * seq, w), BF16), jax.ShapeDtypeStruct((batch, 2, w), F32)),
        grid=(batch,),
        in_specs=[pl.BlockSpec((seq, w), lambda b: (b, COL_LRU_X)),
                  pl.BlockSpec((seq, w), lambda b: (b, COL_LRU_G)),
                  const(4, w), const(1, w), const(HEADS, 128, 4 * 128), const(2, w), const(2, w), const(2, w),
                  pl.BlockSpec((None, 2, w), lambda b: (b, 0, 0))],
        out_specs=(pl.BlockSpec((seq, w), lambda b: (b, 0)),
                   pl.BlockSpec((None, 2, w), lambda b: (b, 0, 0))),
        scratch_shapes=[pltpu.VMEM((seq, w), F32), pltpu.VMEM((4, seq, w), F32)]
        + [pltpu.VMEM((2, w // 128, seq, 128), F32)] * 3,
        compiler_params=_params("parallel"),
        name="rg_lru",
    )(p_main, p_main, conv_w, conv_b, w_gates, b_r, b_i, lam, h0)


def _head_rms(x, w):
    return x * lax.rsqrt(jnp.mean(x * x, axis=-1, keepdims=True) + NORM_EPS) * w


def _rope(x, cos, sin_lo, sin_hi):
    return x * cos + pltpu.roll(x, 96, 1) * sin_lo + pltpu.roll(x, 32, 1) * sin_hi


def _attn_kernel(*refs, n_kv, has_cache, bias_plan, qk_norm, rope, write_kv):
    has_bias = bias_plan is not None
    it = iter(refs)
    q_ref, k_ref, v_ref, g_ref = next(it), next(it), next(it), next(it)
    kc_ref = vc_ref = bias_ref = qn_ref = kn_ref = None
    if has_cache:
        kc_ref, vc_ref = next(it), next(it)
    if has_bias:
        bias_ref = next(it)
    if qk_norm:
        qn_ref, kn_ref = next(it), next(it)
    if rope:
        cq_ref, slq_ref, shq_ref, ck_ref, slk_ref, shk_ref = (next(it) for _ in range(6))
    if write_kv:
        next(it), next(it)
    o_ref = next(it)
    if write_kv:
        ko_ref, vo_ref = next(it), next(it)
    prep_k = qk_norm or rope
    if prep_k:
        kb_s = next(it)
    vb_s = next(it)
    if has_cache:
        kcb_s, vcb_s = next(it), next(it)
    if has_bias:
        bias_s = next(it)
        @pl.when(jnp.logical_and(pl.program_id(0) == 0, pl.program_id(1) == 0))
        def _():
            for t, kind in enumerate(bias_plan.kinds):
                for h in range(HEADS):
                    for a, row in enumerate(kind):
                        for p, entry in enumerate(row):
                            bias_s[t, h, a * GRID_W:(a + 1) * GRID_W, p * 128:(p + 1) * 128] = bias_ref[h, entry]

    @pl.when(pl.program_id(1) == 0)
    def _():
        for kvh in range(n_kv):
            cols = slice(kvh * HEAD_DIM, (kvh + 1) * HEAD_DIM)
            vcols = slice(2 * kvh * HEAD_DIM, (2 * kvh + 1) * HEAD_DIM)
            ocols = slice((2 * kvh + 1) * HEAD_DIM, (2 * kvh + 2) * HEAD_DIM)
            k = k_ref[:, cols]
            v = v_ref[:, cols]
            if qk_norm:
                k = _head_rms(k.astype(F32), kn_ref[...])
            if write_kv:
                ko_ref[pl.ds(kvh, k.shape[0], stride=n_kv), :] = k.astype(F32)
                vo_ref[pl.ds(kvh, v.shape[0], stride=n_kv), :] = v.astype(F32)
            if rope:
                k = _rope(k.astype(F32), ck_ref[...], slk_ref[...], shk_ref[...])
            if prep_k:
                kb_s[:, cols] = k.astype(BF16)
            vb_s[:, vcols] = v.astype(BF16)
            vb_s[:, ocols] = jnp.ones((vb_s.shape[0], HEAD_DIM), BF16)
            if has_cache:
                lc = kcb_s.shape[0]
                kcb_s[:, cols] = kc_ref[pl.ds(kvh, lc, stride=n_kv), :].astype(BF16)
                vcb_s[:, vcols] = vc_ref[pl.ds(kvh, lc, stride=n_kv), :].astype(BF16)
                vcb_s[:, ocols] = jnp.ones((vcb_s.shape[0], HEAD_DIM), BF16)

    c2 = HEAD_DIM ** -0.5 * LOG2E
    group = HEADS // n_kv
    krows = slice(None)
    if has_bias:
        start, kind_id = 0, 0
        for blk in range(len(bias_plan.kind_of)):
            start = jnp.where(pl.program_id(1) == blk, bias_plan.key_start[blk], start)
            kind_id = jnp.where(pl.program_id(1) == blk, bias_plan.kind_of[blk], kind_id)
        krows = pl.ds(pl.multiple_of(start, 256), bias_plan.window)
    s1, s2, p1, p2 = {}, {}, {}, {}
    for h in range(HEADS):
        cols = slice(h // group * HEAD_DIM, (h // group + 1) * HEAD_DIM)
        q = q_ref[:, h * HEAD_DIM:(h + 1) * HEAD_DIM]
        if qk_norm:
            q = _head_rms(q.astype(F32), qn_ref[...])
        if rope:
            q = _rope(q.astype(F32), cq_ref[...], slq_ref[...], shq_ref[...])
        qb = q.astype(BF16)
        s1[h] = _dot_nt(qb, kb_s[krows, cols] if prep_k else k_ref[krows, cols])
        if has_cache:
            s2[h] = _dot_nt(qb, kcb_s[:, cols])
    for h in range(HEADS):
        if has_bias:
            t1 = s1[h] * c2 + bias_s[kind_id, h]
            mx = jnp.max(t1, axis=-1, keepdims=True)
            if has_cache:
                t2 = s2[h] * c2
                mx = jnp.maximum(mx, jnp.max(t2, axis=-1, keepdims=True))
                p2[h] = jnp.exp2(t2 - mx).astype(BF16)
            p1[h] = jnp.exp2(t1 - mx).astype(BF16)
        else:
            mx = jnp.max(s1[h], axis=-1, keepdims=True)
            if has_cache:
                mx = jnp.maximum(mx, jnp.max(s2[h], axis=-1, keepdims=True))
                p2[h] = jnp.exp2((s2[h] - mx) * c2).astype(BF16)
            p1[h] = jnp.exp2((s1[h] - mx) * c2).astype(BF16)
    for h in range(HEADS):
        hc = slice(h * HEAD_DIM, (h + 1) * HEAD_DIM)
        v1cols = slice(2 * (h // group) * HEAD_DIM, (2 * (h // group) + 2) * HEAD_DIM)
        acc = _dot(p1[h], vb_s[krows, v1cols])
        if has_cache:
            acc = acc + _dot(p2[h], vcb_s[:, v1cols])
        o = acc[:, 0:HEAD_DIM] / acc[:, HEAD_DIM:HEAD_DIM + 1]
        o_ref[:, hc] = (o * _silu(g_ref[:, hc].astype(F32))).astype(BF16)


def _attention(p_main, batch, seq, *, col_q, col_g, k_spec, v_spec, n_kv, cache=None, bias=None,
               qk_norm=None, rope=None, kv_out=None):
    tq = 256
    nq = seq // tq
    kvw = n_kv * HEAD_DIM
    in_specs = [pl.BlockSpec((tq, GROUP_W), lambda b, i: (b * nq + i, col_q)),
                pl.BlockSpec((seq, kvw), lambda b, i: (b, k_spec)),
                pl.BlockSpec((seq, kvw), lambda b, i: (b, v_spec)),
                pl.BlockSpec((tq, GROUP_W), lambda b, i: (b * nq + i, col_g))]
    args = [p_main, p_main, p_main, p_main]
    scratch = [pltpu.VMEM((seq, kvw), BF16)] if (qk_norm is not None or rope is not None) else []
    scratch.append(pltpu.VMEM((seq, 2 * kvw), BF16))
    if cache is not None:
        kc, vc, layer = cache
        lc = kc.shape[2] // n_kv
        in_specs += [pl.BlockSpec((None, None, lc * n_kv, HEAD_DIM), lambda b, i: (b, layer, 0, 0))] * 2
        args += [kc, vc]
        scratch += [pltpu.VMEM((lc, kvw), BF16), pltpu.VMEM((lc, 2 * kvw), BF16)]
    bias_plan = None
    if bias is not None:
        pairs, layer_b, bias_plan = bias
        in_specs.append(pl.BlockSpec((None,) + pairs.shape[1:], lambda b, i: (layer_b, 0, 0, 0, 0)))
        args.append(pairs)
        scratch.append(pltpu.VMEM((len(bias_plan.kinds), HEADS, tq, bias_plan.window), F32))
    if qk_norm is not None:
        in_specs += [pl.BlockSpec((1, HEAD_DIM), lambda b, i: (0, 0))] * 2
        args += list(qk_norm)
    if rope is not None:
        in_specs += [pl.BlockSpec((tq, HEAD_DIM), lambda b, i: (i, 0))] * 3
        in_specs += [pl.BlockSpec((seq, HEAD_DIM), lambda b, i: (0, 0))] * 3
        args += list(rope) + list(rope)
    out_shape = [jax.ShapeDtypeStruct((batch * seq, GROUP_W), BF16)]
    out_specs = [pl.BlockSpec((tq, GROUP_W), lambda b, i: (b * nq + i, 0))]
    aliases = {}
    if kv_out is not None:
        assert nq == 1
        k_all, v_all, layer_o = kv_out
        for j, arr in enumerate((k_all, v_all)):
            aliases[len(args)] = 1 + j
            in_specs.append(pl.BlockSpec(memory_space=pl.ANY))
            args.append(arr)
            out_shape.append(jax.ShapeDtypeStruct(arr.shape, arr.dtype))
            out_specs.append(pl.BlockSpec((None, None, seq * n_kv, HEAD_DIM), lambda b, i: (b, layer_o, 0, 0)))
    return pl.pallas_call(
        functools.partial(_attn_kernel, n_kv=n_kv, has_cache=cache is not None, bias_plan=bias_plan,
                          qk_norm=qk_norm is not None, rope=rope is not None, write_kv=kv_out is not None),
        out_shape=tuple(out_shape),
        grid=(batch, nq),
        in_specs=in_specs,
        out_specs=tuple(out_specs),
        scratch_shapes=scratch,
        input_output_aliases=aliases,
        compiler_params=_params("arbitrary", "arbitrary"),
        name="attention",
    )(*args)


def _mlstm_kernel(*refs, seq, hp, has_state, write_state):
    it = iter(refs)
    q_ref, k_ref, v_ref, og_ref, g_ref, gates_ref, gb_ref, onw_ref = (next(it) for _ in range(8))
    if has_state:
        c0_ref, n0_ref, m0_ref = next(it), next(it), next(it)
    if write_state:
        next(it)
    out_ref = next(it)
    if write_state:
        cout_ref, nout_ref, mout_ref = next(it), next(it), next(it)
    ht_s = next(it)

    L = ML_CHUNK
    nc = seq // L
    hd = HEAD_DIM
    scale = hd ** -0.5
    gt = (gates_ref[...] + gb_ref[...]).T[0:4 * HEADS, :]
    gate_id = lax.broadcasted_iota(jnp.int32, gt.shape, 0)
    gt = jnp.where((gate_id // HEADS) % 2 == 1, _log_sigmoid(gt), gt) * LOG2E

    def head_of(j):
        return j if hp == HEADS else pl.program_id(1) * hp + j

    def gate_row(r):
        if isinstance(r, int):
            return gt[r:r + 1, :]
        return jnp.sum(jnp.where(gate_id == r, gt, 0.0), axis=0, keepdims=True)

    si = lax.broadcasted_iota(jnp.int32, (L, L), 0)
    ti = lax.broadcasted_iota(jnp.int32, (L, L), 1)
    pick3 = jnp.where(lax.broadcasted_iota(jnp.int32, (16, hd), 0) < 3, 1.0, 0.0).astype(BF16)
    row16 = lax.broadcasted_iota(jnp.int32, (16, L), 0)

    def stack3(x):
        x1, x2, x3 = (p.astype(F32) for p in _split3(x))
        stacked = jnp.where(row16 == 0, x1, jnp.where(row16 == 1, x2, jnp.where(row16 == 2, x3, 0.0)))
        return stacked.astype(BF16)

    masks = [si <= ti, si >= ti]
    tris = [jnp.where(mk, 1.0, 0.0).astype(BF16) for mk in masks]
    edges = [L - 1, 0]
    units = [(j, d) for j in range(hp) for d in range(2)]
    ii_row, ff_row, state = {}, {}, {}
    for j, d in units:
        h = head_of(j)
        ii_row[j, d] = gate_row(d * 8 + h)
        ff_row[j, d] = gate_row(d * 8 + HEADS + h)
        if has_state:
            state[j, d] = (c0_ref[d, j].T, n0_ref[d, pl.ds(h, 1), :], m0_ref[d, pl.ds(h, 1), 0:1] * LOG2E)
        else:
            state[j, d] = (jnp.zeros((hd, hd), F32), jnp.zeros((1, hd), F32), jnp.zeros((1, 1), F32))

    ones_cols = jnp.ones((L, hd), BF16)

    def load_chunk(j, c):
        hc = slice(j * hd, (j + 1) * hd)
        rws = slice(c * L, (c + 1) * L)
        k = k_ref[rws, hc].astype(F32)
        return ((q_ref[rws, hc].astype(F32) * scale).astype(BF16), k, k.astype(BF16),
                jnp.concatenate([v_ref[rws, hc].astype(BF16), ones_cols], axis=1))

    loaded = {(j, c): load_chunk(j, c) for j in range(hp) for c in range(nc)}
    for step in range(nc):
        chunk = [step, nc - 1 - step]
        rows = {u: slice(chunk[u[1]] * L, (chunk[u[1]] + 1) * L) for u in units}
        b, g, g_col, g_src, mx = {}, {}, {}, {}, {}
        for u in units:
            b3 = _dot(stack3(ff_row[u][:, rows[u]]), tris[u[1]])
            b[u] = b3[0:1, :] + b3[1:2, :] + b3[2:3, :]
            g[u] = ii_row[u][:, rows[u]] - b[u]
        for u in units:
            g_col[u] = _dot_tn(stack3(g[u]), pick3)
        for u in units:
            g_src[u] = jnp.where(masks[u[1]], jnp.concatenate([g_col[u]] * (L // hd), axis=1), NEG_INF)
            mx[u] = jnp.maximum(state[u][2], jnp.max(g_src[u], axis=0, keepdims=True))
        for u in units:
            j, d = u
            ct, n, m = state[u]
            qb, k, kb, v1 = loaded[j, chunk[d]]
            p = _dot_nt(kb, qb) * jnp.exp2(g_src[u] - mx[u])
            w_inter = jnp.exp2(m - mx[u])
            n16 = jnp.broadcast_to(n, (16, hd)).astype(BF16)
            den = jnp.sum(p, axis=0, keepdims=True) + w_inter * _dot_nt(n16, qb)[0:1, :]
            num = _dot_tn(v1[:, 0:hd], p.astype(BF16)) + w_inter * _dot_nt(ct.astype(BF16), qb)
            ht_s[d, j, :, rows[u]] = num / jnp.maximum(jnp.abs(den), jnp.exp2(-(b[u] + mx[u])))
            mx_end = mx[u][:, edges[d]:edges[d] + 1]
            w_prev = jnp.exp2(m - mx_end)
            ku = k * jnp.exp2(g_col[u] - mx_end)
            kv = _dot_tn(v1, ku.astype(BF16))
            state[u] = (w_prev * ct + kv[0:hd, :], w_prev * n + kv[hd:hd + 1, :],
                        b[u][:, edges[d]:edges[d] + 1] + mx_end)

    if write_state:
        for j, d in units:
            ct, n, m = state[j, d]
            cout_ref[d, j] = ct.T
            nout_ref[d, pl.ds(head_of(j), 1), :] = n
            mout_ref[d, pl.ds(head_of(j), 1), :] = jnp.broadcast_to(m * LN2, (1, hd))

    for j in range(hp):
        hc = slice(j * hd, (j + 1) * hd)
        hs = (ht_s[0, j] + ht_s[1, j]).T
        y = hs * lax.rsqrt(jnp.mean(hs * hs, axis=-1, keepdims=True) + NORM_EPS) * onw_ref[:, hc]
        out_ref[:, hc] = (y * _sigmoid(og_ref[:, hc].astype(F32)) * _silu(g_ref[:, hc].astype(F32))).astype(BF16)


def _mlstm(p_main, p_if, batch, seq, gate_b, out_norm_w, hp, state=None, state_out=None):
    hd = HEAD_DIM
    nh = HEADS // hp
    w = hp * hd
    col = lambda base: (lambda b, h: (b, base * nh + h))
    in_specs = [pl.BlockSpec((seq, w), col(COL_ML_Q)), pl.BlockSpec((seq, w), col(COL_ML_K)),
                pl.BlockSpec((seq, w), col(COL_ML_V)), pl.BlockSpec((seq, w), col(COL_ML_O)),
                pl.BlockSpec((seq, w), col(COL_ML_G)),
                pl.BlockSpec((seq, 128), lambda b, h: (b, 0)),
                pl.BlockSpec((1, 128), lambda b, h: (0, 0)),
                pl.BlockSpec((1, w), lambda b, h: (0, h))]
    args = [p_main] * 5 + [p_if, jnp.pad(gate_b, (0, 128 - gate_b.shape[0])).reshape(1, 128),
                           out_norm_w.reshape(1, GROUP_W)]
    if state is not None:
        c0, n0, m0, layer = state
        m0b = jnp.broadcast_to(m0[..., None], m0.shape + (hd,))
        in_specs += [pl.BlockSpec((None, None, 2, hp, hd, hd), lambda b, h: (b, layer, 0, h, 0, 0)),
                     pl.BlockSpec((None, None, 2, HEADS, hd), lambda b, h: (b, layer, 0, 0, 0)),
                     pl.BlockSpec((None, None, 2, HEADS, hd), lambda b, h: (b, layer, 0, 0, 0))]
        args += [c0, n0, m0b]
    out_shape = [jax.ShapeDtypeStruct((batch * seq, GROUP_W), BF16)]
    out_specs = [pl.BlockSpec((seq, w), lambda b, h: (b, h))]
    aliases = {}
    if state_out is not None:
        c_all, layer_o = state_out
        aliases[len(args)] = 1
        in_specs.append(pl.BlockSpec(memory_space=pl.ANY))
        args.append(c_all)
        out_shape += [jax.ShapeDtypeStruct(c_all.shape, c_all.dtype),
                      jax.ShapeDtypeStruct((batch, 2, HEADS, hd), F32),
                      jax.ShapeDtypeStruct((batch, 2, HEADS, hd), F32)]
        out_specs += [pl.BlockSpec((None, None, 2, hp, hd, hd), lambda b, h: (b, layer_o, 0, h, 0, 0)),
                      pl.BlockSpec((None, 2, HEADS, hd), lambda b, h: (b, 0, 0, 0)),
                      pl.BlockSpec((None, 2, HEADS, hd), lambda b, h: (b, 0, 0, 0))]
    return pl.pallas_call(
        functools.partial(_mlstm_kernel, seq=seq, hp=hp, has_state=state is not None,
                          write_state=state_out is not None),
        out_shape=tuple(out_shape),
        grid=(batch, nh),
        in_specs=in_specs,
        out_specs=tuple(out_specs),
        scratch_shapes=[pltpu.VMEM((2, hp, hd, seq), F32)],
        input_output_aliases=aliases,
        compiler_params=_params("parallel", "arbitrary"),
        name="mlstm",
    )(*args)


class BiasPlan(NamedTuple):
    kinds: tuple
    kind_of: tuple
    key_start: tuple
    window: int


NA_Q_ROWS = 4
NA_K_ROWS = 12


def _na_bias(rpb, seq):
    rows = seq // GRID_W
    wr = min(NA_ROWS, rows)
    assert rows % NA_Q_ROWS == 0 and rows >= NA_K_ROWS
    c = np.arange(GRID_W)
    c_start = np.clip(c - NA_COLS // 2, 0, GRID_W - NA_COLS)
    col_ok = (c[None, :] >= c_start[:, None]) & (c[None, :] < c_start[:, None] + NA_COLS)
    col_off = np.clip(c[None, :] - c[:, None] + NA_COLS - 1, 0, 2 * NA_COLS - 2)
    col_sel = (col_off[..., None] == np.arange(2 * NA_COLS - 1)).astype(np.float32)
    blocks = jnp.einsum('lhij,cdj->lhicd', rpb.astype(F32), col_sel, precision=lax.Precision.HIGHEST)
    blocks = jnp.where(col_ok, blocks * LOG2E, NEG_INF)
    blocks = jnp.concatenate([blocks, jnp.full_like(blocks[:, :, :1], NEG_INF)], axis=2)
    masked = 2 * NA_ROWS - 1
    nblk = rows // NA_Q_ROWS
    key_row0 = [int(min(np.clip(i * NA_Q_ROWS - wr // 2, 0, rows - wr), rows - NA_K_ROWS)) for i in range(nblk)]
    idx = np.full((nblk, NA_Q_ROWS, NA_K_ROWS), masked, np.int32)
    for i in range(nblk):
        for a in range(NA_Q_ROWS):
            qr = i * NA_Q_ROWS + a
            r_start = int(np.clip(qr - wr // 2, 0, rows - wr))
            assert key_row0[i] <= r_start and r_start + wr <= key_row0[i] + NA_K_ROWS
            for kk in range(NA_K_ROWS):
                kr = key_row0[i] + kk
                if r_start <= kr < r_start + wr:
                    idx[i, a, kk] = kr - qr + NA_ROWS - 1
    kinds, kind_of = np.unique(idx.reshape(nblk, -1), axis=0, return_inverse=True)
    pair_of = kinds.reshape(-1, NA_Q_ROWS, NA_K_ROWS // 2, 2)
    pairs, entry = np.unique(pair_of.reshape(-1, 2), axis=0, return_inverse=True)
    table = jnp.concatenate([blocks[:, :, pairs[:, 0]], blocks[:, :, pairs[:, 1]]], axis=-1)
    entry = np.ravel(entry).reshape(pair_of.shape[:3])
    plan = BiasPlan(kinds=tuple(tuple(tuple(int(e) for e in row) for row in kind) for kind in entry),
                    kind_of=tuple(int(t) for t in np.ravel(kind_of)),
                    key_start=tuple(r * GRID_W for r in key_row0), window=NA_K_ROWS * GRID_W)
    return table, plan


def _rope_tables(seq):
    t = jnp.arange(seq)
    nf = HEAD_DIM // 4
    inv = ROPE_THETA ** (-jnp.arange(nf, dtype=F32) / nf)
    zeros = jnp.zeros((seq, nf), F32)

    def half(pos):
        ang = pos.astype(F32)[:, None] * inv
        cs, sn = jnp.cos(ang), jnp.sin(ang)
        return jnp.concatenate([cs, cs], -1), jnp.concatenate([-sn, zeros], -1), jnp.concatenate([zeros, sn], -1)

    parts = [half(t // GRID_W), half(t % GRID_W)]
    return tuple(jnp.concatenate([parts[0][i], parts[1][i]], -1) for i in range(3))


def kernel(x_prompt, x_sample, cache_na_k, cache_na_v, cache_gqa_k, cache_gqa_v, state_lru, state_mlstm_C,
           state_mlstm_n, state_mlstm_m, c, c_ctx, norm_w, ada_w, ada_b, w_in, lru_conv_w, lru_conv_b, lru_wr,
           lru_br, lru_wi, lru_bi, lru_lambda, na_rpb, gqa_qnorm, gqa_knorm, ml_gate_b, ml_out_norm, w_out,
           final_norm_w):
    bp, tp, d = x_prompt.shape
    bs, ts, _ = x_sample.shape
    depth = w_in.shape[0]
    past = cache_na_k.shape[2]

    cond = jnp.zeros((COND_ROWS, d), F32).at[0].set(c_ctx).at[1:1 + bs].set(c)
    mod4 = _modulation(cond, ada_w, ada_b).reshape(depth, COND_ROWS, 1, 3 * d)

    w_in_t = jnp.swapaxes(w_in, 1, 2)
    w_out_b = w_out.astype(BF16)
    w_gates = jnp.concatenate([lru_wr[:, 0], lru_wi[:, 0], lru_wr[:, 1], lru_wi[:, 1]], axis=-1).astype(BF16)
    bias_pairs, bias_plan = _na_bias(na_rpb, ts)
    rope = _rope_tables(ts)
    zeros_h0 = jnp.zeros((bp, 2, GROUP_W), F32)

    xp = x_prompt.reshape(bp * tp, d)
    xs = x_sample.reshape(bs * ts, d)
    norm_w2 = norm_w.reshape(depth, 1, d)
    fnw = final_norm_w.reshape(1, d)
    na_k = jnp.zeros((bp, depth, tp * HEADS, HEAD_DIM), F32)
    na_v = jnp.zeros((bp, depth, tp * HEADS, HEAD_DIM), F32)
    gqa_k = jnp.zeros((bp, depth, tp * GQA_KV_HEADS, HEAD_DIM), F32)
    gqa_v = jnp.zeros((bp, depth, tp * GQA_KV_HEADS, HEAD_DIM), F32)
    st_c = jnp.zeros((bp, depth, 2, HEADS, HEAD_DIM, HEAD_DIM), F32)
    flat = lambda a: a.reshape(a.shape[0], a.shape[1], a.shape[2] * a.shape[3], a.shape[4])
    cache_na = (flat(cache_na_k), flat(cache_na_v))
    cache_gqa = (flat(cache_gqa_k), flat(cache_gqa_v))
    new = {k: [] for k in ('lru', 'n', 'm')}

    for l in range(depth):
        last = l == depth - 1
        lru_args = (lru_conv_w[l], lru_conv_b[l].reshape(1, GROUP_W), w_gates[l], lru_br[l], lru_bi[l], lru_lambda[l])
        qk_norm = (gqa_qnorm[l].reshape(1, HEAD_DIM), gqa_knorm[l].reshape(1, HEAD_DIM))

        p, pif = _in_projection(xp, norm_w2[l], mod4, l, 0, bp * tp, w_in_t)
        out_a, st_lru = _lru(p, bp, tp, *lru_args, zeros_h0)
        out_b, na_k, na_v = _attention(p, bp, tp, col_q=COL_NA_Q, col_g=COL_NA_G, k_spec=COL_NA_K,
                                       v_spec=COL_NA_V, n_kv=HEADS, kv_out=(na_k, na_v, l))
        out_c, gqa_k, gqa_v = _attention(p, bp, tp, col_q=COL_GQA_Q, col_g=COL_GQA_G, k_spec=2 * COL_GQA_KV,
                                         v_spec=2 * COL_GQA_KV + 1, n_kv=GQA_KV_HEADS, qk_norm=qk_norm,
                                         kv_out=(gqa_k, gqa_v, l))
        out_d, st_c, st_n, st_m = _mlstm(p, pif, bp, tp, ml_gate_b[l], ml_out_norm[l], HEADS, state_out=(st_c, l))
        xp = _out_projection((out_a, out_b, out_c, out_d), w_out_b[l], xp, mod4, l, 0, bp * tp,
                             fnw if last else None)
        new['lru'].append(st_lru)
        new['n'].append(st_n)
        new['m'].append(st_m[..., 0])

        p, pif = _in_projection(xs, norm_w2[l], mod4, l, 1, ts, w_in_t)
        out_a, _ = _lru(p, bs, ts, *lru_args, state_lru[:, l])
        (out_b,) = _attention(p, bs, ts, col_q=COL_NA_Q, col_g=COL_NA_G, k_spec=COL_NA_K, v_spec=COL_NA_V,
                              n_kv=HEADS, cache=cache_na + (l,), bias=(bias_pairs, l, bias_plan))
        (out_c,) = _attention(p, bs, ts, col_q=COL_GQA_Q, col_g=COL_GQA_G, k_spec=2 * COL_GQA_KV,
                              v_spec=2 * COL_GQA_KV + 1, n_kv=GQA_KV_HEADS, cache=cache_gqa + (l,),
                              qk_norm=qk_norm, rope=rope)
        (out_d,) = _mlstm(p, pif, bs, ts, ml_gate_b[l], ml_out_norm[l], HEADS,
                          state=(state_mlstm_C, state_mlstm_n, state_mlstm_m, l))
        xs = _out_projection((out_a, out_b, out_c, out_d), w_out_b[l], xs, mod4, l, 1, ts,
                             fnw if last else None)

    stacked = {k: jnp.stack(v, axis=1) for k, v in new.items()}
    heads = lambda a, n: a.reshape(bp, depth, tp, n, HEAD_DIM)
    return (xp.reshape(bp, tp, d), xs.reshape(bs, ts, d), heads(na_k, HEADS), heads(na_v, HEADS),
            heads(gqa_k, GQA_KV_HEADS), heads(gqa_v, GQA_KV_HEADS),
            stacked['lru'], st_c, stacked['n'], stacked['m'])
```

```python
import functools
from typing import NamedTuple

import numpy as np
import jax
import jax.numpy as jnp
from jax import lax
from jax.experimental import pallas as pl
from jax.experimental.pallas import tpu as pltpu

F32 = jnp.float32
BF16 = jnp.bfloat16

HEAD_DIM = 128
GROUP_W = 512
HEADS = GROUP_W // HEAD_DIM
GQA_KV_HEADS = 2
GRID_W = 64
NORM_EPS = 1e-6
NEG_INF = -1e30
LOG2E = 1.4426950408889634
LN2 = 0.6931471805599453
LRU_C = 8.0
NA_ROWS, NA_COLS = 8, 16
ROPE_THETA = 10000.0
ML_CHUNK = 256
COND_ROWS = 16
ROW_CHUNK = 256
CTX_ATTN_BATCH = 4
SCAN_BLOCK = 64
W_CAST_CHUNKS = 8
VMEM_LIMIT = 52 * 1024 * 1024

COL_LRU_X, COL_LRU_G, COL_NA_Q, COL_NA_K, COL_NA_V, COL_NA_G = 0, 1, 2, 3, 4, 5
COL_GQA_Q, COL_GQA_KV, COL_GQA_G = 6, 7, 8
COL_ML_Q, COL_ML_K, COL_ML_V, COL_ML_O, COL_ML_G = 9, 10, 11, 12, 13
MAIN_W = 14 * GROUP_W


def _params(*sem):
    return pltpu.CompilerParams(dimension_semantics=sem, vmem_limit_bytes=VMEM_LIMIT)


def _sigmoid(x):
    return 1.0 / (1.0 + jnp.exp(-x))


def _silu(x):
    return x * _sigmoid(x)


def _log_sigmoid(x):
    return jnp.minimum(x, 0.0) - jnp.log1p(jnp.exp(-jnp.abs(x)))


def _dot(a, b):
    return jnp.dot(a, b, preferred_element_type=F32)


def _dot_nt(a, b):
    return lax.dot_general(a, b, (((1,), (1,)), ((), ())), preferred_element_type=F32)


def _dot_tn(a, b):
    return lax.dot_general(a, b, (((0,), (0,)), ((), ())), preferred_element_type=F32)


def _split3(x):
    x1 = x.astype(BF16)
    r = x - x1.astype(F32)
    x2 = r.astype(BF16)
    x3 = (r - x2.astype(F32)).astype(BF16)
    return x1, x2, x3


def _mod_kernel(cond_ref, w_ref, b_ref, o_ref):
    s = _silu(cond_ref[...]).astype(BF16)
    o_ref[...] = _dot(s, w_ref[...].astype(BF16)) + b_ref[...]


def _modulation(cond, ada_w, ada_b):
    depth, d, n = ada_w.shape
    tn = 1024
    return pl.pallas_call(
        _mod_kernel,
        out_shape=jax.ShapeDtypeStruct((depth, COND_ROWS, n), F32),
        grid=(depth, n // tn),
        in_specs=[pl.BlockSpec((COND_ROWS, d), lambda l, j: (0, 0)),
                  pl.BlockSpec((None, d, tn), lambda l, j: (l, 0, j)),
                  pl.BlockSpec((None, 1, tn), lambda l, j: (l, 0, j))],
        out_specs=pl.BlockSpec((None, COND_ROWS, tn), lambda l, j: (l, 0, j)),
        compiler_params=_params("parallel", "parallel"),
        name="adaln_mod",
    )(cond, ada_w, ada_b.reshape(depth, 1, n))


def _inproj_kernel(x_ref, nw_ref, sh_ref, sc_ref, w_ref, wif_ref, o_ref, oif_ref, xm_ref, *, n_if):
    @pl.when(pl.program_id(1) == 0)
    def _():
        gain = nw_ref[...] * (1.0 + sc_ref[...])
        for r in range(x_ref.shape[0] // ROW_CHUNK):
            rows = slice(r * ROW_CHUNK, (r + 1) * ROW_CHUNK)
            x = x_ref[rows, :]
            inv = lax.rsqrt(jnp.mean(x * x, axis=-1, keepdims=True) + NORM_EPS)
            xm_ref[rows, :] = (x * inv * gain + sh_ref[...]).astype(BF16)
        wif = jnp.concatenate([wif_ref[...], jnp.zeros((128 - n_if, wif_ref.shape[1]), F32)], axis=0)
        oif_ref[...] = _dot_nt(xm_ref[...], wif.astype(BF16))

    kc = w_ref.shape[1] // W_CAST_CHUNKS
    acc = _dot_nt(xm_ref[:, 0:kc], w_ref[:, 0:kc].astype(BF16))
    for c in range(1, W_CAST_CHUNKS):
        acc += _dot_nt(xm_ref[:, c * kc:(c + 1) * kc], w_ref[:, c * kc:(c + 1) * kc].astype(BF16))
    o_ref[...] = acc.astype(o_ref.dtype)


def _in_projection(x, norm_w, mod4, layer, mod_row0, rows_per_mod, w_in_t):
    m, d = x.shape
    tm, tn = min(1024, m), 1024
    n_if = w_in_t.shape[1] - MAIN_W
    assert m % tm == 0 and rows_per_mod % tm == 0 and n_if % 8 == 0 and MAIN_W % n_if == 0 and n_if <= 128
    per_batch = rows_per_mod < m

    def mod_map(col):
        if per_batch:
            return lambda i, j: (layer, mod_row0 + (i * tm) // rows_per_mod, 0, col)
        return lambda i, j: (layer, mod_row0, 0, col)

    return pl.pallas_call(
        functools.partial(_inproj_kernel, n_if=n_if),
        out_shape=(jax.ShapeDtypeStruct((m, MAIN_W), BF16), jax.ShapeDtypeStruct((m, 128), F32)),
        grid=(m // tm, MAIN_W // tn),
        in_specs=[pl.BlockSpec((tm, d), lambda i, j: (i, 0)),
                  pl.BlockSpec((1, d), lambda i, j: (0, 0)),
                  pl.BlockSpec((None, None, 1, d), mod_map(0)),
                  pl.BlockSpec((None, None, 1, d), mod_map(1)),
                  pl.BlockSpec((None, tn, d), lambda i, j: (layer, j, 0)),
                  pl.BlockSpec((None, n_if, d), lambda i, j: (layer, MAIN_W // n_if, 0))],
        out_specs=(pl.BlockSpec((tm, tn), lambda i, j: (i, j)),
                   pl.BlockSpec((tm, 128), lambda i, j: (i, 0))),
        scratch_shapes=[pltpu.VMEM((tm, d), BF16)],
        compiler_params=_params("parallel", "arbitrary"),
        name="in_proj",
    )(x, norm_w, mod4, mod4, w_in_t, w_in_t)


def _outproj_kernel(a_ref, b_ref, c_ref, d_ref, w_ref, x_ref, g_ref, *rest, final):
    o_ref = rest[-1]
    for n in range(w_ref.shape[1] // GROUP_W):
        cols = slice(n * GROUP_W, (n + 1) * GROUP_W)
        acc = _dot(a_ref[...], w_ref[0:GROUP_W, cols])
        acc += _dot(b_ref[...], w_ref[GROUP_W:2 * GROUP_W, cols])
        acc += _dot(c_ref[...], w_ref[2 * GROUP_W:3 * GROUP_W, cols])
        acc += _dot(d_ref[...], w_ref[3 * GROUP_W:4 * GROUP_W, cols])
        o_ref[:, cols] = x_ref[:, cols] + g_ref[:, cols] * acc
    if final:
        fnw_ref = rest[0]
        for r in range(o_ref.shape[0] // ROW_CHUNK):
            rows = slice(r * ROW_CHUNK, (r + 1) * ROW_CHUNK)
            xn = o_ref[rows, :]
            o_ref[rows, :] = xn * lax.rsqrt(jnp.mean(xn * xn, axis=-1, keepdims=True) + NORM_EPS) * fnw_ref[...]


def _out_projection(branches, w_out, x, mod4, layer, mod_row0, rows_per_mod, final_norm_w):
    m, d = x.shape
    tm = 512
    per_batch = rows_per_mod < m
    if per_batch:
        gate_map = lambda i: (layer, mod_row0 + (i * tm) // rows_per_mod, 0, 2)
    else:
        gate_map = lambda i: (layer, mod_row0, 0, 2)
    final = final_norm_w is not None
    in_specs = [pl.BlockSpec((tm, GROUP_W), lambda i: (i, 0)) for _ in range(4)]
    in_specs += [pl.BlockSpec((d, d), lambda i: (0, 0)),
                 pl.BlockSpec((tm, d), lambda i: (i, 0)),
                 pl.BlockSpec((None, None, 1, d), gate_map)]
    args = list(branches) + [w_out, x, mod4]
    if final:
        in_specs.append(pl.BlockSpec((1, d), lambda i: (0, 0)))
        args.append(final_norm_w)
    return pl.pallas_call(
        functools.partial(_outproj_kernel, final=final),
        out_shape=jax.ShapeDtypeStruct((m, d), F32),
        grid=(m // tm,),
        in_specs=in_specs,
        out_specs=pl.BlockSpec((tm, d), lambda i: (i, 0)),
        compiler_params=_params("parallel"),
        name="out_proj",
    )(*args)


def _scan8(a, b, row, reverse):
    for dist in (1, 2, 4):
        if reverse:
            keep = row < 8 - dist
            shift = 8 - dist
        else:
            keep = row >= dist
            shift = dist
        a_sh = jnp.where(keep, pltpu.roll(a, shift, 0), 1.0)
        b_sh = jnp.where(keep, pltpu.roll(b, shift, 0), 0.0)
        b = b + a * b_sh
        a = a * a_sh
    return a, b


def _lru_kernel(x_ref, g_ref, cw_ref, cb_ref, wg_ref, br_ref, bi_ref, lam_ref, h0_ref,
                o_ref, st_ref, xc_s, pre_s, a_s, b_s, h_s, *, seq):
    w = GROUP_W
    nlb = w // 128
    rc = 256
    x = x_ref[...].astype(F32)
    t = lax.broadcasted_iota(jnp.int32, (seq, w), 0)
    xc = x * cw_ref[1:2, :] + cb_ref[...]
    xc += jnp.where(t >= 1, pltpu.roll(x, 1, 0), 0.0) * cw_ref[0:1, :]
    xc += jnp.where(t < seq - 1, pltpu.roll(x, seq - 1, 0), 0.0) * cw_ref[2:3, :]
    xc += jnp.where(t < seq - 2, pltpu.roll(x, seq - 2, 0), 0.0) * cw_ref[3:4, :]
    xc_s[...] = xc
    for g in range(HEADS):
        pre = _dot(xc_s[:, g * 128:(g + 1) * 128].astype(BF16), wg_ref[g])
        for k in range(4):
            pre_s[k, :, g * 128:(g + 1) * 128] = pre[:, k * 128:(k + 1) * 128]
    for d in range(2):
        log_lam = LRU_C * _log_sigmoid(lam_ref[d:d + 1, :])
        for c in range(seq // rc):
            rows = slice(c * rc, (c + 1) * rc)
            r = _sigmoid(pre_s[2 * d, rows, :] + br_ref[d:d + 1, :])
            i = _sigmoid(pre_s[2 * d + 1, rows, :] + bi_ref[d:d + 1, :])
            log_a = log_lam * r
            a = jnp.exp(log_a)
            b = jnp.sqrt(-jnp.tanh(log_a) * (a * a + 1.0)) * i * xc_s[rows, :]
            for lb in range(nlb):
                a_s[d, lb, rows, :] = a[:, lb * 128:(lb + 1) * 128]
                b_s[d, lb, rows, :] = b[:, lb * 128:(lb + 1) * 128]

    row = lax.broadcasted_iota(jnp.int32, (8, 128), 0)
    nblocks = seq // SCAN_BLOCK

    def body(g, carry):
        units = [(d, lb) for d in range(2) for lb in range(nlb)]
        r0 = [pl.multiple_of(g * SCAN_BLOCK, SCAN_BLOCK), pl.multiple_of((nblocks - 1 - g) * SCAN_BLOCK, SCAN_BLOCK)]
        hs = {u: [None] * 8 for u in units}
        ps = {u: [None] * 8 for u in units}
        for step in range(8):
            for u in units:
                d, lb = u
                j, prev = (step, step - 1) if d == 0 else (7 - step, 8 - step)
                a = a_s[d, lb, pl.ds(r0[d] + j, 8, stride=8), :]
                b = b_s[d, lb, pl.ds(r0[d] + j, 8, stride=8), :]
                hs[u][j] = b if step == 0 else a * hs[u][prev] + b
                ps[u][j] = a if step == 0 else a * ps[u][prev]
        new = []
        for i, u in enumerate(units):
            d, lb = u
            tot = 7 if d == 0 else 0
            a_seg, b_seg = _scan8(ps[u][tot], hs[u][tot], row, d == 1)
            end = b_seg + a_seg * carry[i]
            if d == 0:
                enter = jnp.where(row >= 1, pltpu.roll(end, 1, 0), carry[i])
            else:
                enter = jnp.where(row < 7, pltpu.roll(end, 7, 0), carry[i])
            for j in range(8):
                h_s[d, lb, pl.ds(r0[d] + j, 8, stride=8), :] = hs[u][j] + ps[u][j] * enter
            new.append(jnp.broadcast_to(end[7:8, :] if d == 0 else end[0:1, :], (8, 128)))
        return tuple(new)

    init = tuple(jnp.broadcast_to(h0_ref[d:d + 1, lb * 128:(lb + 1) * 128], (8, 128))
                 for d in range(2) for lb in range(nlb))
    final = lax.fori_loop(0, nblocks, body, init)
    for d in range(2):
        for lb in range(nlb):
            st_ref[d:d + 1, lb * 128:(lb + 1) * 128] = final[d * nlb + lb][0:1, :]
    for c in range(seq // rc):
        rows = slice(c * rc, (c + 1) * rc)
        for lb in range(nlb):
            cols = slice(lb * 128, (lb + 1) * 128)
            h_sum = h_s[0, lb, rows, :] + h_s[1, lb, rows, :]
            o_ref[rows, cols] = (h_sum * _silu(g_ref[rows, cols].astype(F32))).astype(BF16)


def _lru(p_main, batch, seq, conv_w, conv_b, w_gates, b_r, b_i, lam, h0):
    w = GROUP_W
    const = lambda *shape: pl.BlockSpec(shape, lambda b: (0,) * len(shape))
    return pl.pallas_call(
        functools.partial(_lru_kernel, seq=seq),
        out_shape=(jax.ShapeDtypeStruct((batch * seq, w), BF16), jax.ShapeDtypeStruct((batch, 2, w), F32)),
        grid=(batch,),
        in_specs=[pl.BlockSpec((seq, w), lambda b: (b, COL_LRU_X)),
                  pl.BlockSpec((seq, w), lambda b: (b, COL_LRU_G)),
                  const(4, w), const(1, w), const(HEADS, 128, 4 * 128), const(2, w), const(2, w), const(2, w),
                  pl.BlockSpec((None, 2, w), lambda b: (b, 0, 0))],
        out_specs=(pl.BlockSpec((seq, w), lambda b: (b, 0)),
                   pl.BlockSpec((None, 2, w), lambda b: (b, 0, 0))),
        scratch_shapes=[pltpu.VMEM((seq, w), F32), pltpu.VMEM((4, seq, w), F32)]
        + [pltpu.VMEM((2, w // 128, seq, 128), F32)] * 3,
        compiler_params=_params("parallel"),
        name="rg_lru",
    )(p_main, p_main, conv_w, conv_b, w_gates, b_r, b_i, lam, h0)


def _head_rms(x, w):
    return x * lax.rsqrt(jnp.mean(x * x, axis=-1, keepdims=True) + NORM_EPS) * w


def _rope(x, cos, sin_lo, sin_hi):
    return x * cos + pltpu.roll(x, 96, 1) * sin_lo + pltpu.roll(x, 32, 1) * sin_hi


def _attn_kernel(*refs, bb, **kw):
    for bi in range(bb):
        _attn_element(*refs, bb=bb, bi=bi, **kw)


def _attn_element(*refs, n_kv, has_cache, bias_plan, qk_norm, rope, write_kv, n_alias, bb, bi):
    has_bias = bias_plan is not None
    it = iter(refs)
    q_ref, k_ref, v_ref, g_ref = next(it), next(it), next(it), next(it)
    kc_ref = vc_ref = bias_ref = qn_ref = kn_ref = None
    if has_cache:
        kc_ref, vc_ref = next(it), next(it)
    if has_bias:
        bias_ref = next(it)
    if qk_norm:
        qn_ref, kn_ref = next(it), next(it)
    if rope:
        cq_ref, slq_ref, shq_ref, ck_ref, slk_ref, shk_ref = (next(it) for _ in range(6))
    for _ in range(n_alias):
        next(it)
    o_ref = next(it)
    if write_kv:
        ko_ref, vo_ref = next(it), next(it)
    prep_k = qk_norm or rope
    if prep_k:
        kb_s = next(it)
    vb_s = next(it)
    if has_cache:
        kcb_s, vcb_s = next(it), next(it)
    if bb > 1:
        n_rows = q_ref.shape[0] // bb
        rows = pl.ds(bi * n_rows, n_rows)
        q_ref, k_ref, v_ref, g_ref, o_ref, vb_s = (r.at[rows] for r in (q_ref, k_ref, v_ref, g_ref, o_ref, vb_s))
        if prep_k:
            kb_s = kb_s.at[rows]
        if write_kv:
            ko_ref, vo_ref = ko_ref.at[bi], vo_ref.at[bi]
    if has_bias:
        bias_s = next(it)
        @pl.when(jnp.logical_and(pl.program_id(0) == 0, pl.program_id(1) == 0))
        def _():
            for t, kind in enumerate(bias_plan.kinds):
                for h in range(HEADS):
                    for a, row in enumerate(kind):
                        for p, entry in enumerate(row):
                            bias_s[t, h, a * GRID_W:(a + 1) * GRID_W, p * 128:(p + 1) * 128] = bias_ref[h, entry]

    @pl.when(pl.program_id(1) == 0)
    def _():
        for kvh in range(n_kv):
            cols = slice(kvh * HEAD_DIM, (kvh + 1) * HEAD_DIM)
            vcols = slice(2 * kvh * HEAD_DIM, (2 * kvh + 1) * HEAD_DIM)
            ocols = slice((2 * kvh + 1) * HEAD_DIM, (2 * kvh + 2) * HEAD_DIM)
            k = k_ref[:, cols]
            v = v_ref[:, cols]
            if qk_norm:
                k = _head_rms(k.astype(F32), kn_ref[...])
            if write_kv:
                ko_ref[pl.ds(kvh, k.shape[0], stride=n_kv), :] = k.astype(F32)
                vo_ref[pl.ds(kvh, v.shape[0], stride=n_kv), :] = v.astype(F32)
            if rope:
                k = _rope(k.astype(F32), ck_ref[...], slk_ref[...], shk_ref[...])
            if prep_k:
                kb_s[:, cols] = k.astype(BF16)
            vb_s[:, vcols] = v.astype(BF16)
            vb_s[:, ocols] = jnp.ones((vb_s.shape[0], HEAD_DIM), BF16)
            if has_cache:
                lc = kcb_s.shape[0]
                kcb_s[:, cols] = kc_ref[pl.ds(kvh, lc, stride=n_kv), :].astype(BF16)
                vcb_s[:, vcols] = vc_ref[pl.ds(kvh, lc, stride=n_kv), :].astype(BF16)
                vcb_s[:, ocols] = jnp.ones((vcb_s.shape[0], HEAD_DIM), BF16)

    c2 = HEAD_DIM ** -0.5 * LOG2E
    group = HEADS // n_kv
    krows = slice(None)
    if has_bias:
        start, kind_id = 0, 0
        for blk in range(len(bias_plan.kind_of)):
            start = jnp.where(pl.program_id(1) == blk, bias_plan.key_start[blk], start)
            kind_id = jnp.where(pl.program_id(1) == blk, bias_plan.kind_of[blk], kind_id)
        krows = pl.ds(pl.multiple_of(start, 256), bias_plan.window)
    s1, s2, p1, p2 = {}, {}, {}, {}
    for h in range(HEADS):
        cols = slice(h // group * HEAD_DIM, (h // group + 1) * HEAD_DIM)
        q = q_ref[:, h * HEAD_DIM:(h + 1) * HEAD_DIM]
        if qk_norm:
            q = _head_rms(q.astype(F32), qn_ref[...])
        if rope:
            q = _rope(q.astype(F32), cq_ref[...], slq_ref[...], shq_ref[...])
        qb = q.astype(BF16)
        s1[h] = _dot_nt(qb, kb_s[krows, cols] if prep_k else k_ref[krows, cols])
        if has_cache:
            s2[h] = _dot_nt(qb, kcb_s[:, cols])
    for h in range(HEADS):
        if has_bias:
            t1 = s1[h] * c2 + bias_s[kind_id, h]
            mx = jnp.max(t1, axis=-1, keepdims=True)
            if has_cache:
                t2 = s2[h] * c2
                mx = jnp.maximum(mx, jnp.max(t2, axis=-1, keepdims=True))
                p2[h] = jnp.exp2(t2 - mx).astype(BF16)
            p1[h] = jnp.exp2(t1 - mx).astype(BF16)
        else:
            mx = jnp.max(s1[h], axis=-1, keepdims=True)
            if has_cache:
                mx = jnp.maximum(mx, jnp.max(s2[h], axis=-1, keepdims=True))
                p2[h] = jnp.exp2((s2[h] - mx) * c2).astype(BF16)
            p1[h] = jnp.exp2((s1[h] - mx) * c2).astype(BF16)
    for h in range(HEADS):
        hc = slice(h * HEAD_DIM, (h + 1) * HEAD_DIM)
        v1cols = slice(2 * (h // group) * HEAD_DIM, (2 * (h // group) + 2) * HEAD_DIM)
        acc = _dot(p1[h], vb_s[krows, v1cols])
        if has_cache:
            acc = acc + _dot(p2[h], vcb_s[:, v1cols])
        o = acc[:, 0:HEAD_DIM] / acc[:, HEAD_DIM:HEAD_DIM + 1]
        o_ref[:, hc] = (o * _silu(g_ref[:, hc].astype(F32))).astype(BF16)


def _attention(p_main, batch, seq, *, col_q, col_g, k_spec, v_spec, n_kv, cache=None, bias=None,
               qk_norm=None, rope=None, kv_out=None, bb=1):
    tq = 256
    nq = seq // tq
    kvw = n_kv * HEAD_DIM
    assert batch % bb == 0 and (bb == 1 or (nq == 1 and cache is None and bias is None and rope is None))
    in_specs = [pl.BlockSpec((bb * tq, GROUP_W), lambda b, i: (b * nq + i, col_q)),
                pl.BlockSpec((bb * seq, kvw), lambda b, i: (b, k_spec)),
                pl.BlockSpec((bb * seq, kvw), lambda b, i: (b, v_spec)),
                pl.BlockSpec((bb * tq, GROUP_W), lambda b, i: (b * nq + i, col_g))]
    args = [p_main, p_main, p_main, p_main]
    scratch = [pltpu.VMEM((bb * seq, kvw), BF16)] if (qk_norm is not None or rope is not None) else []
    scratch.append(pltpu.VMEM((bb * seq, 2 * kvw), BF16))
    if cache is not None:
        kc, vc, layer = cache
        lc = kc.shape[2] // n_kv
        in_specs += [pl.BlockSpec((None, None, lc * n_kv, HEAD_DIM), lambda b, i: (b, layer, 0, 0))] * 2
        args += [kc, vc]
        scratch += [pltpu.VMEM((lc, kvw), BF16), pltpu.VMEM((lc, 2 * kvw), BF16)]
    bias_plan = None
    if bias is not None:
        pairs, layer_b, bias_plan = bias
        in_specs.append(pl.BlockSpec((None,) + pairs.shape[1:], lambda b, i: (layer_b, 0, 0, 0, 0)))
        args.append(pairs)
        scratch.append(pltpu.VMEM((len(bias_plan.kinds), HEADS, tq, bias_plan.window), F32))
    if qk_norm is not None:
        in_specs += [pl.BlockSpec((1, HEAD_DIM), lambda b, i: (0, 0))] * 2
        args += list(qk_norm)
    if rope is not None:
        in_specs += [pl.BlockSpec((tq, HEAD_DIM), lambda b, i: (i, 0))] * 3
        in_specs += [pl.BlockSpec((seq, HEAD_DIM), lambda b, i: (0, 0))] * 3
        args += list(rope) + list(rope)
    out_shape = [jax.ShapeDtypeStruct((batch * seq, GROUP_W), BF16)]
    out_specs = [pl.BlockSpec((bb * tq, GROUP_W), lambda b, i: (b * nq + i, 0))]
    aliases = {}
    if kv_out is not None:
        assert nq == 1
        k_all, v_all, layer_o = kv_out
        for j, arr in enumerate((k_all, v_all)):
            if not isinstance(arr, jax.ShapeDtypeStruct):
                aliases[len(args)] = 1 + j
                in_specs.append(pl.BlockSpec(memory_space=pl.ANY))
                args.append(arr)
            out_shape.append(jax.ShapeDtypeStruct(arr.shape, arr.dtype))
            out_specs.append(pl.BlockSpec((None if bb == 1 else bb, None, seq * n_kv, HEAD_DIM),
                                          lambda b, i: (b, layer_o, 0, 0)))
    return pl.pallas_call(
        functools.partial(_attn_kernel, n_kv=n_kv, has_cache=cache is not None, bias_plan=bias_plan,
                          qk_norm=qk_norm is not None, rope=rope is not None, write_kv=kv_out is not None,
                          n_alias=len(aliases), bb=bb),
        out_shape=tuple(out_shape),
        grid=(batch // bb, nq),
        in_specs=in_specs,
        out_specs=tuple(out_specs),
        scratch_shapes=scratch,
        input_output_aliases=aliases,
        compiler_params=_params("arbitrary", "arbitrary"),
        name="attention",
    )(*args)


def _mlstm_kernel(*refs, seq, hp, has_state, write_state, n_alias):
    it = iter(refs)
    q_ref, k_ref, v_ref, og_ref, g_ref, gates_ref, gb_ref, onw_ref = (next(it) for _ in range(8))
    if has_state:
        c0_ref, n0_ref, m0_ref = next(it), next(it), next(it)
    for _ in range(n_alias):
        next(it)
    out_ref = next(it)
    if write_state:
        cout_ref, nout_ref, mout_ref = next(it), next(it), next(it)
    ht_s = next(it)

    L = ML_CHUNK
    nc = seq // L
    hd = HEAD_DIM
    scale = hd ** -0.5
    gt = (gates_ref[...] + gb_ref[...]).T[0:4 * HEADS, :]
    gate_id = lax.broadcasted_iota(jnp.int32, gt.shape, 0)
    gt = jnp.where((gate_id // HEADS) % 2 == 1, _log_sigmoid(gt), gt) * LOG2E

    def head_of(j):
        return j if hp == HEADS else pl.program_id(1) * hp + j

    def gate_row(r):
        if isinstance(r, int):
            return gt[r:r + 1, :]
        return jnp.sum(jnp.where(gate_id == r, gt, 0.0), axis=0, keepdims=True)

    si = lax.broadcasted_iota(jnp.int32, (L, L), 0)
    ti = lax.broadcasted_iota(jnp.int32, (L, L), 1)
    pick3 = jnp.where(lax.broadcasted_iota(jnp.int32, (16, hd), 0) < 3, 1.0, 0.0).astype(BF16)
    row16 = lax.broadcasted_iota(jnp.int32, (16, L), 0)

    def stack3(x):
        x1, x2, x3 = (p.astype(F32) for p in _split3(x))
        stacked = jnp.where(row16 == 0, x1, jnp.where(row16 == 1, x2, jnp.where(row16 == 2, x3, 0.0)))
        return stacked.astype(BF16)

    masks = [si <= ti, si >= ti]
    tris = [jnp.where(mk, 1.0, 0.0).astype(BF16) for mk in masks]
    edges = [L - 1, 0]
    units = [(j, d) for j in range(hp) for d in range(2)]
    ii_row, ff_row, state = {}, {}, {}
    for j, d in units:
        h = head_of(j)
        ii_row[j, d] = gate_row(d * 8 + h)
        ff_row[j, d] = gate_row(d * 8 + HEADS + h)
        if has_state:
            state[j, d] = (c0_ref[d, j].T, n0_ref[d, pl.ds(h, 1), :], m0_ref[d, pl.ds(h, 1), 0:1] * LOG2E)
        else:
            state[j, d] = (jnp.zeros((hd, hd), F32), jnp.zeros((1, hd), F32), jnp.zeros((1, 1), F32))

    ones_cols = jnp.ones((L, hd), BF16)

    def load_chunk(j, c):
        hc = slice(j * hd, (j + 1) * hd)
        rws = slice(c * L, (c + 1) * L)
        k = k_ref[rws, hc].astype(F32)
        return ((q_ref[rws, hc].astype(F32) * scale).astype(BF16), k, k.astype(BF16),
                jnp.concatenate([v_ref[rws, hc].astype(BF16), ones_cols], axis=1))

    loaded = {(j, c): load_chunk(j, c) for j in range(hp) for c in range(nc)}
    for step in range(nc):
        chunk = [step, nc - 1 - step]
        rows = {u: slice(chunk[u[1]] * L, (chunk[u[1]] + 1) * L) for u in units}
        b, g, g_col, g_src, mx = {}, {}, {}, {}, {}
        for u in units:
            b3 = _dot(stack3(ff_row[u][:, rows[u]]), tris[u[1]])
            b[u] = b3[0:1, :] + b3[1:2, :] + b3[2:3, :]
            g[u] = ii_row[u][:, rows[u]] - b[u]
        for u in units:
            g_col[u] = _dot_tn(stack3(g[u]), pick3)
        for u in units:
            g_src[u] = jnp.where(masks[u[1]], jnp.concatenate([g_col[u]] * (L // hd), axis=1), NEG_INF)
            mx[u] = jnp.maximum(state[u][2], jnp.max(g_src[u], axis=0, keepdims=True))
        for u in units:
            j, d = u
            ct, n, m = state[u]
            qb, k, kb, v1 = loaded[j, chunk[d]]
            p = _dot_nt(kb, qb) * jnp.exp2(g_src[u] - mx[u])
            w_inter = jnp.exp2(m - mx[u])
            n16 = jnp.broadcast_to(n, (16, hd)).astype(BF16)
            den = jnp.sum(p, axis=0, keepdims=True) + w_inter * _dot_nt(n16, qb)[0:1, :]
            num = _dot_tn(v1[:, 0:hd], p.astype(BF16)) + w_inter * _dot_nt(ct.astype(BF16), qb)
            ht_s[d, j, :, rows[u]] = num / jnp.maximum(jnp.abs(den), jnp.exp2(-(b[u] + mx[u])))
            mx_end = mx[u][:, edges[d]:edges[d] + 1]
            w_prev = jnp.exp2(m - mx_end)
            ku = k * jnp.exp2(g_col[u] - mx_end)
            kv = _dot_tn(v1, ku.astype(BF16))
            state[u] = (w_prev * ct + kv[0:hd, :], w_prev * n + kv[hd:hd + 1, :],
                        b[u][:, edges[d]:edges[d] + 1] + mx_end)

    if write_state:
        for j, d in units:
            ct, n, m = state[j, d]
            cout_ref[d, j] = ct.T
            nout_ref[d, pl.ds(head_of(j), 1), :] = n
            mout_ref[d, pl.ds(head_of(j), 1), :] = jnp.broadcast_to(m * LN2, (1, hd))

    for j in range(hp):
        hc = slice(j * hd, (j + 1) * hd)
        hs = (ht_s[0, j] + ht_s[1, j]).T
        y = hs * lax.rsqrt(jnp.mean(hs * hs, axis=-1, keepdims=True) + NORM_EPS) * onw_ref[:, hc]
        out_ref[:, hc] = (y * _sigmoid(og_ref[:, hc].astype(F32)) * _silu(g_ref[:, hc].astype(F32))).astype(BF16)


def _mlstm(p_main, p_if, batch, seq, gate_b, out_norm_w, hp, state=None, state_out=None):
    hd = HEAD_DIM
    nh = HEADS // hp
    w = hp * hd
    col = lambda base: (lambda b, h: (b, base * nh + h))
    in_specs = [pl.BlockSpec((seq, w), col(COL_ML_Q)), pl.BlockSpec((seq, w), col(COL_ML_K)),
                pl.BlockSpec((seq, w), col(COL_ML_V)), pl.BlockSpec((seq, w), col(COL_ML_O)),
                pl.BlockSpec((seq, w), col(COL_ML_G)),
                pl.BlockSpec((seq, 128), lambda b, h: (b, 0)),
                pl.BlockSpec((1, 128), lambda b, h: (0, 0)),
                pl.BlockSpec((1, w), lambda b, h: (0, h))]
    args = [p_main] * 5 + [p_if, jnp.pad(gate_b, (0, 128 - gate_b.shape[0])).reshape(1, 128),
                           out_norm_w.reshape(1, GROUP_W)]
    if state is not None:
        c0, n0, m0, layer = state
        m0b = jnp.broadcast_to(m0[..., None], m0.shape + (hd,))
        in_specs += [pl.BlockSpec((None, None, 2, hp, hd, hd), lambda b, h: (b, layer, 0, h, 0, 0)),
                     pl.BlockSpec((None, None, 2, HEADS, hd), lambda b, h: (b, layer, 0, 0, 0)),
                     pl.BlockSpec((None, None, 2, HEADS, hd), lambda b, h: (b, layer, 0, 0, 0))]
        args += [c0, n0, m0b]
    out_shape = [jax.ShapeDtypeStruct((batch * seq, GROUP_W), BF16)]
    out_specs = [pl.BlockSpec((seq, w), lambda b, h: (b, h))]
    aliases = {}
    if state_out is not None:
        c_all, layer_o = state_out
        if not isinstance(c_all, jax.ShapeDtypeStruct):
            aliases[len(args)] = 1
            in_specs.append(pl.BlockSpec(memory_space=pl.ANY))
            args.append(c_all)
        out_shape += [jax.ShapeDtypeStruct(c_all.shape, c_all.dtype),
                      jax.ShapeDtypeStruct((batch, 2, HEADS, hd), F32),
                      jax.ShapeDtypeStruct((batch, 2, HEADS, hd), F32)]
        out_specs += [pl.BlockSpec((None, None, 2, hp, hd, hd), lambda b, h: (b, layer_o, 0, h, 0, 0)),
                      pl.BlockSpec((None, 2, HEADS, hd), lambda b, h: (b, 0, 0, 0)),
                      pl.BlockSpec((None, 2, HEADS, hd), lambda b, h: (b, 0, 0, 0))]
    return pl.pallas_call(
        functools.partial(_mlstm_kernel, seq=seq, hp=hp, has_state=state is not None,
                          write_state=state_out is not None, n_alias=len(aliases)),
        out_shape=tuple(out_shape),
        grid=(batch, nh),
        in_specs=in_specs,
        out_specs=tuple(out_specs),
        scratch_shapes=[pltpu.VMEM((2, hp, hd, seq), F32)],
        input_output_aliases=aliases,
        compiler_params=_params("parallel", "arbitrary"),
        name="mlstm",
    )(*args)


class BiasPlan(NamedTuple):
    kinds: tuple
    kind_of: tuple
    key_start: tuple
    window: int


NA_Q_ROWS = 4
NA_K_ROWS = 12


def _na_bias(rpb, seq):
    rows = seq // GRID_W
    wr = min(NA_ROWS, rows)
    assert rows % NA_Q_ROWS == 0 and rows >= NA_K_ROWS
    c = np.arange(GRID_W)
    c_start = np.clip(c - NA_COLS // 2, 0, GRID_W - NA_COLS)
    col_ok = (c[None, :] >= c_start[:, None]) & (c[None, :] < c_start[:, None] + NA_COLS)
    col_off = np.clip(c[None, :] - c[:, None] + NA_COLS - 1, 0, 2 * NA_COLS - 2)
    col_sel = (col_off[..., None] == np.arange(2 * NA_COLS - 1)).astype(np.float32)
    blocks = jnp.einsum('lhij,cdj->lhicd', rpb.astype(F32), col_sel, precision=lax.Precision.HIGHEST)
    blocks = jnp.where(col_ok, blocks * LOG2E, NEG_INF)
    blocks = jnp.concatenate([blocks, jnp.full_like(blocks[:, :, :1], NEG_INF)], axis=2)
    masked = 2 * NA_ROWS - 1
    nblk = rows // NA_Q_ROWS
    key_row0 = [int(min(np.clip(i * NA_Q_ROWS - wr // 2, 0, rows - wr), rows - NA_K_ROWS)) for i in range(nblk)]
    idx = np.full((nblk, NA_Q_ROWS, NA_K_ROWS), masked, np.int32)
    for i in range(nblk):
        for a in range(NA_Q_ROWS):
            qr = i * NA_Q_ROWS + a
            r_start = int(np.clip(qr - wr // 2, 0, rows - wr))
            assert key_row0[i] <= r_start and r_start + wr <= key_row0[i] + NA_K_ROWS
            for kk in range(NA_K_ROWS):
                kr = key_row0[i] + kk
                if r_start <= kr < r_start + wr:
                    idx[i, a, kk] = kr - qr + NA_ROWS - 1
    kinds, kind_of = np.unique(idx.reshape(nblk, -1), axis=0, return_inverse=True)
    pair_of = kinds.reshape(-1, NA_Q_ROWS, NA_K_ROWS // 2, 2)
    pairs, entry = np.unique(pair_of.reshape(-1, 2), axis=0, return_inverse=True)
    table = jnp.concatenate([blocks[:, :, pairs[:, 0]], blocks[:, :, pairs[:, 1]]], axis=-1)
    entry = np.ravel(entry).reshape(pair_of.shape[:3])
    plan = BiasPlan(kinds=tuple(tuple(tuple(int(e) for e in row) for row in kind) for kind in entry),
                    kind_of=tuple(int(t) for t in np.ravel(kind_of)),
                    key_start=tuple(r * GRID_W for r in key_row0), window=NA_K_ROWS * GRID_W)
    return table, plan


def _rope_tables(seq):
    t = jnp.arange(seq)
    nf = HEAD_DIM // 4
    inv = ROPE_THETA ** (-jnp.arange(nf, dtype=F32) / nf)
    zeros = jnp.zeros((seq, nf), F32)

    def half(pos):
        ang = pos.astype(F32)[:, None] * inv
        cs, sn = jnp.cos(ang), jnp.sin(ang)
        return jnp.concatenate([cs, cs], -1), jnp.concatenate([-sn, zeros], -1), jnp.concatenate([zeros, sn], -1)

    parts = [half(t // GRID_W), half(t % GRID_W)]
    return tuple(jnp.concatenate([parts[0][i], parts[1][i]], -1) for i in range(3))


def kernel(x_prompt, x_sample, cache_na_k, cache_na_v, cache_gqa_k, cache_gqa_v, state_lru, state_mlstm_C,
           state_mlstm_n, state_mlstm_m, c, c_ctx, norm_w, ada_w, ada_b, w_in, lru_conv_w, lru_conv_b, lru_wr,
           lru_br, lru_wi, lru_bi, lru_lambda, na_rpb, gqa_qnorm, gqa_knorm, ml_gate_b, ml_out_norm, w_out,
           final_norm_w):
    bp, tp, d = x_prompt.shape
    bs, ts, _ = x_sample.shape
    depth = w_in.shape[0]
    past = cache_na_k.shape[2]

    cond = jnp.zeros((COND_ROWS, d), F32).at[0].set(c_ctx).at[1:1 + bs].set(c)
    mod4 = _modulation(cond, ada_w, ada_b).reshape(depth, COND_ROWS, 1, 3 * d)

    w_in_t = jnp.swapaxes(w_in, 1, 2)
    w_out_b = w_out.astype(BF16)
    w_gates = jnp.concatenate([lru_wr[:, 0], lru_wi[:, 0], lru_wr[:, 1], lru_wi[:, 1]], axis=-1).astype(BF16)
    bias_pairs, bias_plan = _na_bias(na_rpb, ts)
    rope = _rope_tables(ts)
    zeros_h0 = jnp.zeros((bp, 2, GROUP_W), F32)

    xp = x_prompt.reshape(bp * tp, d)
    xs = x_sample.reshape(bs * ts, d)
    norm_w2 = norm_w.reshape(depth, 1, d)
    fnw = final_norm_w.reshape(1, d)
    na_k = na_v = jax.ShapeDtypeStruct((bp, depth, tp * HEADS, HEAD_DIM), F32)
    gqa_k = gqa_v = jax.ShapeDtypeStruct((bp, depth, tp * GQA_KV_HEADS, HEAD_DIM), F32)
    st_c = jax.ShapeDtypeStruct((bp, depth, 2, HEADS, HEAD_DIM, HEAD_DIM), F32)
    flat = lambda a: a.reshape(a.shape[0], a.shape[1], a.shape[2] * a.shape[3], a.shape[4])
    cache_na = (flat(cache_na_k), flat(cache_na_v))
    cache_gqa = (flat(cache_gqa_k), flat(cache_gqa_v))
    new = {k: [] for k in ('lru', 'n', 'm')}

    for l in range(depth):
        last = l == depth - 1
        lru_args = (lru_conv_w[l], lru_conv_b[l].reshape(1, GROUP_W), w_gates[l], lru_br[l], lru_bi[l], lru_lambda[l])
        qk_norm = (gqa_qnorm[l].reshape(1, HEAD_DIM), gqa_knorm[l].reshape(1, HEAD_DIM))

        p, pif = _in_projection(xp, norm_w2[l], mod4, l, 0, bp * tp, w_in_t)
        out_a, st_lru = _lru(p, bp, tp, *lru_args, zeros_h0)
        out_b, na_k, na_v = _attention(p, bp, tp, col_q=COL_NA_Q, col_g=COL_NA_G, k_spec=COL_NA_K,
                                       v_spec=COL_NA_V, n_kv=HEADS, kv_out=(na_k, na_v, l), bb=CTX_ATTN_BATCH)
        out_c, gqa_k, gqa_v = _attention(p, bp, tp, col_q=COL_GQA_Q, col_g=COL_GQA_G, k_spec=2 * COL_GQA_KV,
                                         v_spec=2 * COL_GQA_KV + 1, n_kv=GQA_KV_HEADS, qk_norm=qk_norm,
                                         kv_out=(gqa_k, gqa_v, l), bb=CTX_ATTN_BATCH)
        out_d, st_c, st_n, st_m = _mlstm(p, pif, bp, tp, ml_gate_b[l], ml_out_norm[l], HEADS, state_out=(st_c, l))
        xp = _out_projection((out_a, out_b, out_c, out_d), w_out_b[l], xp, mod4, l, 0, bp * tp,
                             fnw if last else None)
        new['lru'].append(st_lru)
        new['n'].append(st_n)
        new['m'].append(st_m[..., 0])

        p, pif = _in_projection(xs, norm_w2[l], mod4, l, 1, ts, w_in_t)
        out_a, _ = _lru(p, bs, ts, *lru_args, state_lru[:, l])
        (out_b,) = _attention(p, bs, ts, col_q=COL_NA_Q, col_g=COL_NA_G, k_spec=COL_NA_K, v_spec=COL_NA_V,
                              n_kv=HEADS, cache=cache_na + (l,), bias=(bias_pairs, l, bias_plan))
        (out_c,) = _attention(p, bs, ts, col_q=COL_GQA_Q, col_g=COL_GQA_G, k_spec=2 * COL_GQA_KV,
                              v_spec=2 * COL_GQA_KV + 1, n_kv=GQA_KV_HEADS, cache=cache_gqa + (l,),
                              qk_norm=qk_norm, rope=rope)
        (out_d,) = _mlstm(p, pif, bs, ts, ml_gate_b[l], ml_out_norm[l], HEADS,
                          state=(state_mlstm_C, state_mlstm_n, state_mlstm_m, l))
        xs = _out_projection((out_a, out_b, out_c, out_d), w_out_b[l], xs, mod4, l, 1, ts,
                             fnw if last else None)

    stacked = {k: jnp.stack(v, axis=1) for k, v in new.items()}
    heads = lambda a, n: a.reshape(bp, depth, tp, n, HEAD_DIM)
    return (xp.reshape(bp, tp, d), xs.reshape(bs, ts, d), heads(na_k, HEADS), heads(na_v, HEADS),
            heads(gqa_k, GQA_KV_HEADS), heads(gqa_v, GQA_KV_HEADS),
            stacked['lru'], st_c, stacked['n'], stacked['m'])
```

```python
import functools
from typing import NamedTuple

import numpy as np
import jax
import jax.numpy as jnp
from jax import lax
from jax.experimental import pallas as pl
from jax.experimental.pallas import tpu as pltpu

F32 = jnp.float32
BF16 = jnp.bfloat16

HEAD_DIM = 128
GROUP_W = 512
HEADS = GROUP_W // HEAD_DIM
GQA_KV_HEADS = 2
GRID_W = 64
NORM_EPS = 1e-6
NEG_INF = -1e30
LOG2E = 1.4426950408889634
LN2 = 0.6931471805599453
LRU_C = 8.0
NA_ROWS, NA_COLS = 8, 16
ROPE_THETA = 10000.0
ML_CHUNK = 256
COND_ROWS = 16
ROW_CHUNK = 256
CTX_LRU_BATCH = 2
CTX_ATTN_BATCH = 4
SCAN_BLOCK = 64
W_CAST_CHUNKS = 8
VMEM_LIMIT = 52 * 1024 * 1024

COL_LRU_X, COL_LRU_G, COL_NA_Q, COL_NA_K, COL_NA_V, COL_NA_G = 0, 1, 2, 3, 4, 5
COL_GQA_Q, COL_GQA_KV, COL_GQA_G = 6, 7, 8
COL_ML_Q, COL_ML_K, COL_ML_V, COL_ML_O, COL_ML_G = 9, 10, 11, 12, 13
MAIN_W = 14 * GROUP_W


def _params(*sem):
    return pltpu.CompilerParams(dimension_semantics=sem, vmem_limit_bytes=VMEM_LIMIT)


def _sigmoid(x):
    return 1.0 / (1.0 + jnp.exp(-x))


def _silu(x):
    return x * _sigmoid(x)


def _log_sigmoid(x):
    return jnp.minimum(x, 0.0) - jnp.log1p(jnp.exp(-jnp.abs(x)))


def _dot(a, b):
    return jnp.dot(a, b, preferred_element_type=F32)


def _dot_nt(a, b):
    return lax.dot_general(a, b, (((1,), (1,)), ((), ())), preferred_element_type=F32)


def _dot_tn(a, b):
    return lax.dot_general(a, b, (((0,), (0,)), ((), ())), preferred_element_type=F32)


def _split3(x):
    x1 = x.astype(BF16)
    r = x - x1.astype(F32)
    x2 = r.astype(BF16)
    x3 = (r - x2.astype(F32)).astype(BF16)
    return x1, x2, x3


def _mod_kernel(cond_ref, w_ref, b_ref, o_ref):
    s = _silu(cond_ref[...]).astype(BF16)
    o_ref[...] = _dot(s, w_ref[...].astype(BF16)) + b_ref[...]


def _modulation(cond, ada_w, ada_b):
    depth, d, n = ada_w.shape
    tn = 1024
    return pl.pallas_call(
        _mod_kernel,
        out_shape=jax.ShapeDtypeStruct((depth, COND_ROWS, n), F32),
        grid=(depth, n // tn),
        in_specs=[pl.BlockSpec((COND_ROWS, d), lambda l, j: (0, 0)),
                  pl.BlockSpec((None, d, tn), lambda l, j: (l, 0, j)),
                  pl.BlockSpec((None, 1, tn), lambda l, j: (l, 0, j))],
        out_specs=pl.BlockSpec((None, COND_ROWS, tn), lambda l, j: (l, 0, j)),
        compiler_params=_params("parallel", "parallel"),
        name="adaln_mod",
    )(cond, ada_w, ada_b.reshape(depth, 1, n))


def _inproj_kernel(x_ref, nw_ref, sh_ref, sc_ref, w_ref, wif_ref, o_ref, oif_ref, xm_ref, *, n_if):
    @pl.when(pl.program_id(1) == 0)
    def _():
        gain = nw_ref[...] * (1.0 + sc_ref[...])
        for r in range(x_ref.shape[0] // ROW_CHUNK):
            rows = slice(r * ROW_CHUNK, (r + 1) * ROW_CHUNK)
            x = x_ref[rows, :]
            inv = lax.rsqrt(jnp.mean(x * x, axis=-1, keepdims=True) + NORM_EPS)
            xm_ref[rows, :] = (x * inv * gain + sh_ref[...]).astype(BF16)
        wif = jnp.concatenate([wif_ref[...], jnp.zeros((128 - n_if, wif_ref.shape[1]), F32)], axis=0)
        oif_ref[...] = _dot_nt(xm_ref[...], wif.astype(BF16))

    kc = w_ref.shape[1] // W_CAST_CHUNKS
    acc = _dot_nt(xm_ref[:, 0:kc], w_ref[:, 0:kc].astype(BF16))
    for c in range(1, W_CAST_CHUNKS):
        acc += _dot_nt(xm_ref[:, c * kc:(c + 1) * kc], w_ref[:, c * kc:(c + 1) * kc].astype(BF16))
    o_ref[...] = acc.astype(o_ref.dtype)


def _in_projection(x, norm_w, mod4, layer, mod_row0, rows_per_mod, w_in_t):
    m, d = x.shape
    tm, tn = min(1024, m), 1024
    n_if = w_in_t.shape[1] - MAIN_W
    assert m % tm == 0 and rows_per_mod % tm == 0 and n_if % 8 == 0 and MAIN_W % n_if == 0 and n_if <= 128
    per_batch = rows_per_mod < m

    def mod_map(col):
        if per_batch:
            return lambda i, j: (layer, mod_row0 + (i * tm) // rows_per_mod, 0, col)
        return lambda i, j: (layer, mod_row0, 0, col)

    return pl.pallas_call(
        functools.partial(_inproj_kernel, n_if=n_if),
        out_shape=(jax.ShapeDtypeStruct((m, MAIN_W), BF16), jax.ShapeDtypeStruct((m, 128), F32)),
        grid=(m // tm, MAIN_W // tn),
        in_specs=[pl.BlockSpec((tm, d), lambda i, j: (i, 0)),
                  pl.BlockSpec((1, d), lambda i, j: (0, 0)),
                  pl.BlockSpec((None, None, 1, d), mod_map(0)),
                  pl.BlockSpec((None, None, 1, d), mod_map(1)),
                  pl.BlockSpec((None, tn, d), lambda i, j: (layer, j, 0)),
                  pl.BlockSpec((None, n_if, d), lambda i, j: (layer, MAIN_W // n_if, 0))],
        out_specs=(pl.BlockSpec((tm, tn), lambda i, j: (i, j)),
                   pl.BlockSpec((tm, 128), lambda i, j: (i, 0))),
        scratch_shapes=[pltpu.VMEM((tm, d), BF16)],
        compiler_params=_params("parallel", "arbitrary"),
        name="in_proj",
    )(x, norm_w, mod4, mod4, w_in_t, w_in_t)


def _outproj_kernel(a_ref, b_ref, c_ref, d_ref, w_ref, x_ref, g_ref, *rest, final):
    o_ref = rest[-1]
    for n in range(w_ref.shape[1] // GROUP_W):
        cols = slice(n * GROUP_W, (n + 1) * GROUP_W)
        acc = _dot(a_ref[...], w_ref[0:GROUP_W, cols])
        acc += _dot(b_ref[...], w_ref[GROUP_W:2 * GROUP_W, cols])
        acc += _dot(c_ref[...], w_ref[2 * GROUP_W:3 * GROUP_W, cols])
        acc += _dot(d_ref[...], w_ref[3 * GROUP_W:4 * GROUP_W, cols])
        o_ref[:, cols] = x_ref[:, cols] + g_ref[:, cols] * acc
    if final:
        fnw_ref = rest[0]
        for r in range(o_ref.shape[0] // ROW_CHUNK):
            rows = slice(r * ROW_CHUNK, (r + 1) * ROW_CHUNK)
            xn = o_ref[rows, :]
            o_ref[rows, :] = xn * lax.rsqrt(jnp.mean(xn * xn, axis=-1, keepdims=True) + NORM_EPS) * fnw_ref[...]


def _out_projection(branches, w_out, x, mod4, layer, mod_row0, rows_per_mod, final_norm_w):
    m, d = x.shape
    tm = 512
    per_batch = rows_per_mod < m
    if per_batch:
        gate_map = lambda i: (layer, mod_row0 + (i * tm) // rows_per_mod, 0, 2)
    else:
        gate_map = lambda i: (layer, mod_row0, 0, 2)
    final = final_norm_w is not None
    in_specs = [pl.BlockSpec((tm, GROUP_W), lambda i: (i, 0)) for _ in range(4)]
    in_specs += [pl.BlockSpec((d, d), lambda i: (0, 0)),
                 pl.BlockSpec((tm, d), lambda i: (i, 0)),
                 pl.BlockSpec((None, None, 1, d), gate_map)]
    args = list(branches) + [w_out, x, mod4]
    if final:
        in_specs.append(pl.BlockSpec((1, d), lambda i: (0, 0)))
        args.append(final_norm_w)
    return pl.pallas_call(
        functools.partial(_outproj_kernel, final=final),
        out_shape=jax.ShapeDtypeStruct((m, d), F32),
        grid=(m // tm,),
        in_specs=in_specs,
        out_specs=pl.BlockSpec((tm, d), lambda i: (i, 0)),
        compiler_params=_params("parallel"),
        name="out_proj",
    )(*args)


def _scan8(a, b, row, reverse):
    for dist in (1, 2, 4):
        if reverse:
            keep = row < 8 - dist
            shift = 8 - dist
        else:
            keep = row >= dist
            shift = dist
        a_sh = jnp.where(keep, pltpu.roll(a, shift, 0), 1.0)
        b_sh = jnp.where(keep, pltpu.roll(b, shift, 0), 0.0)
        b = b + a * b_sh
        a = a * a_sh
    return a, b


def _lru_kernel(x_ref, g_ref, cw_ref, cb_ref, wg_ref, br_ref, bi_ref, lam_ref, h0_ref, o_ref, st_ref, *scratch,
                seq, bb):
    for e in range(bb):
        rows = pl.ds(e * seq, seq)
        _lru_element(x_ref.at[rows], g_ref.at[rows], cw_ref, cb_ref, wg_ref, br_ref, bi_ref, lam_ref, h0_ref.at[e],
                     o_ref.at[rows], st_ref.at[e], *scratch, seq=seq)


def _lru_element(x_ref, g_ref, cw_ref, cb_ref, wg_ref, br_ref, bi_ref, lam_ref, h0_ref,
                 o_ref, st_ref, xc_s, pre_s, a_s, b_s, h_s, *, seq):
    w = GROUP_W
    nlb = w // 128
    rc = 256
    x = x_ref[...].astype(F32)
    t = lax.broadcasted_iota(jnp.int32, (seq, w), 0)
    xc = x * cw_ref[1:2, :] + cb_ref[...]
    xc += jnp.where(t >= 1, pltpu.roll(x, 1, 0), 0.0) * cw_ref[0:1, :]
    xc += jnp.where(t < seq - 1, pltpu.roll(x, seq - 1, 0), 0.0) * cw_ref[2:3, :]
    xc += jnp.where(t < seq - 2, pltpu.roll(x, seq - 2, 0), 0.0) * cw_ref[3:4, :]
    xc_s[...] = xc
    for g in range(HEADS):
        pre = _dot(xc_s[:, g * 128:(g + 1) * 128].astype(BF16), wg_ref[g])
        for k in range(4):
            pre_s[k, :, g * 128:(g + 1) * 128] = pre[:, k * 128:(k + 1) * 128]
    for d in range(2):
        log_lam = LRU_C * _log_sigmoid(lam_ref[d:d + 1, :])
        for c in range(seq // rc):
            rows = slice(c * rc, (c + 1) * rc)
            r = _sigmoid(pre_s[2 * d, rows, :] + br_ref[d:d + 1, :])
            i = _sigmoid(pre_s[2 * d + 1, rows, :] + bi_ref[d:d + 1, :])
            log_a = log_lam * r
            a = jnp.exp(log_a)
            b = jnp.sqrt(-jnp.tanh(log_a) * (a * a + 1.0)) * i * xc_s[rows, :]
            for lb in range(nlb):
                a_s[d, lb, rows, :] = a[:, lb * 128:(lb + 1) * 128]
                b_s[d, lb, rows, :] = b[:, lb * 128:(lb + 1) * 128]

    row = lax.broadcasted_iota(jnp.int32, (8, 128), 0)
    nblocks = seq // SCAN_BLOCK

    def body(g, carry):
        units = [(d, lb) for d in range(2) for lb in range(nlb)]
        r0 = [pl.multiple_of(g * SCAN_BLOCK, SCAN_BLOCK), pl.multiple_of((nblocks - 1 - g) * SCAN_BLOCK, SCAN_BLOCK)]
        hs = {u: [None] * 8 for u in units}
        ps = {u: [None] * 8 for u in units}
        for step in range(8):
            for u in units:
                d, lb = u
                j, prev = (step, step - 1) if d == 0 else (7 - step, 8 - step)
                a = a_s[d, lb, pl.ds(r0[d] + j, 8, stride=8), :]
                b = b_s[d, lb, pl.ds(r0[d] + j, 8, stride=8), :]
                hs[u][j] = b if step == 0 else a * hs[u][prev] + b
                ps[u][j] = a if step == 0 else a * ps[u][prev]
        new = []
        for i, u in enumerate(units):
            d, lb = u
            tot = 7 if d == 0 else 0
            a_seg, b_seg = _scan8(ps[u][tot], hs[u][tot], row, d == 1)
            end = b_seg + a_seg * carry[i]
            if d == 0:
                enter = jnp.where(row >= 1, pltpu.roll(end, 1, 0), carry[i])
            else:
                enter = jnp.where(row < 7, pltpu.roll(end, 7, 0), carry[i])
            for j in range(8):
                h_s[d, lb, pl.ds(r0[d] + j, 8, stride=8), :] = hs[u][j] + ps[u][j] * enter
            new.append(jnp.broadcast_to(end[7:8, :] if d == 0 else end[0:1, :], (8, 128)))
        return tuple(new)

    init = tuple(jnp.broadcast_to(h0_ref[d:d + 1, lb * 128:(lb + 1) * 128], (8, 128))
                 for d in range(2) for lb in range(nlb))
    final = lax.fori_loop(0, nblocks, body, init)
    for d in range(2):
        for lb in range(nlb):
            st_ref[d:d + 1, lb * 128:(lb + 1) * 128] = final[d * nlb + lb][0:1, :]
    for c in range(seq // rc):
        rows = slice(c * rc, (c + 1) * rc)
        for lb in range(nlb):
            cols = slice(lb * 128, (lb + 1) * 128)
            h_sum = h_s[0, lb, rows, :] + h_s[1, lb, rows, :]
            o_ref[rows, cols] = (h_sum * _silu(g_ref[rows, cols].astype(F32))).astype(BF16)


def _lru(p_main, batch, seq, conv_w, conv_b, w_gates, b_r, b_i, lam, h0, bb=1):
    w = GROUP_W
    assert batch % bb == 0
    const = lambda *shape: pl.BlockSpec(shape, lambda b: (0,) * len(shape))
    return pl.pallas_call(
        functools.partial(_lru_kernel, seq=seq, bb=bb),
        out_shape=(jax.ShapeDtypeStruct((batch * seq, w), BF16), jax.ShapeDtypeStruct((batch, 2, w), F32)),
        grid=(batch // bb,),
        in_specs=[pl.BlockSpec((bb * seq, w), lambda b: (b, COL_LRU_X)),
                  pl.BlockSpec((bb * seq, w), lambda b: (b, COL_LRU_G)),
                  const(4, w), const(1, w), const(HEADS, 128, 4 * 128), const(2, w), const(2, w), const(2, w),
                  pl.BlockSpec((bb, 2, w), lambda b: (b, 0, 0))],
        out_specs=(pl.BlockSpec((bb * seq, w), lambda b: (b, 0)),
                   pl.BlockSpec((bb, 2, w), lambda b: (b, 0, 0))),
        scratch_shapes=[pltpu.VMEM((seq, w), F32), pltpu.VMEM((4, seq, w), F32)]
        + [pltpu.VMEM((2, w // 128, seq, 128), F32)] * 3,
        compiler_params=_params("parallel"),
        name="rg_lru",
    )(p_main, p_main, conv_w, conv_b, w_gates, b_r, b_i, lam, h0)


def _head_rms(x, w):
    return x * lax.rsqrt(jnp.mean(x * x, axis=-1, keepdims=True) + NORM_EPS) * w


def _rope(x, cos, sin_lo, sin_hi):
    return x * cos + pltpu.roll(x, 96, 1) * sin_lo + pltpu.roll(x, 32, 1) * sin_hi


def _attn_kernel(*refs, bb, **kw):
    for bi in range(bb):
        _attn_element(*refs, bb=bb, bi=bi, **kw)


def _attn_element(*refs, n_kv, has_cache, bias_plan, qk_norm, rope, write_kv, n_alias, bb, bi):
    has_bias = bias_plan is not None
    it = iter(refs)
    q_ref, k_ref, v_ref, g_ref = next(it), next(it), next(it), next(it)
    kc_ref = vc_ref = bias_ref = qn_ref = kn_ref = None
    if has_cache:
        kc_ref, vc_ref = next(it), next(it)
    if has_bias:
        bias_ref = next(it)
    if qk_norm:
        qn_ref, kn_ref = next(it), next(it)
    if rope:
        cq_ref, slq_ref, shq_ref, ck_ref, slk_ref, shk_ref = (next(it) for _ in range(6))
    for _ in range(n_alias):
        next(it)
    o_ref = next(it)
    if write_kv:
        ko_ref, vo_ref = next(it), next(it)
    prep_k = qk_norm or rope
    if prep_k:
        kb_s = next(it)
    vb_s = next(it)
    if has_cache:
        kcb_s, vcb_s = next(it), next(it)
    if bb > 1:
        n_rows = q_ref.shape[0] // bb
        rows = pl.ds(bi * n_rows, n_rows)
        q_ref, k_ref, v_ref, g_ref, o_ref, vb_s = (r.at[rows] for r in (q_ref, k_ref, v_ref, g_ref, o_ref, vb_s))
        if prep_k:
            kb_s = kb_s.at[rows]
        if write_kv:
            ko_ref, vo_ref = ko_ref.at[bi], vo_ref.at[bi]
    if has_bias:
        bias_s = next(it)
        @pl.when(jnp.logical_and(pl.program_id(0) == 0, pl.program_id(1) == 0))
        def _():
            for t, kind in enumerate(bias_plan.kinds):
                for h in range(HEADS):
                    for a, row in enumerate(kind):
                        for p, entry in enumerate(row):
                            bias_s[t, h, a * GRID_W:(a + 1) * GRID_W, p * 128:(p + 1) * 128] = bias_ref[h, entry]

    @pl.when(pl.program_id(1) == 0)
    def _():
        for kvh in range(n_kv):
            cols = slice(kvh * HEAD_DIM, (kvh + 1) * HEAD_DIM)
            vcols = slice(2 * kvh * HEAD_DIM, (2 * kvh + 1) * HEAD_DIM)
            ocols = slice((2 * kvh + 1) * HEAD_DIM, (2 * kvh + 2) * HEAD_DIM)
            k = k_ref[:, cols]
            v = v_ref[:, cols]
            if qk_norm:
                k = _head_rms(k.astype(F32), kn_ref[...])
            if write_kv:
                ko_ref[pl.ds(kvh, k.shape[0], stride=n_kv), :] = k.astype(F32)
                vo_ref[pl.ds(kvh, v.shape[0], stride=n_kv), :] = v.astype(F32)
            if rope:
                k = _rope(k.astype(F32), ck_ref[...], slk_ref[...], shk_ref[...])
            if prep_k:
                kb_s[:, cols] = k.astype(BF16)
            vb_s[:, vcols] = v.astype(BF16)
            vb_s[:, ocols] = jnp.ones((vb_s.shape[0], HEAD_DIM), BF16)
            if has_cache:
                lc = kcb_s.shape[0]
                kcb_s[:, cols] = kc_ref[pl.ds(kvh, lc, stride=n_kv), :].astype(BF16)
                vcb_s[:, vcols] = vc_ref[pl.ds(kvh, lc, stride=n_kv), :].astype(BF16)
                vcb_s[:, ocols] = jnp.ones((vcb_s.shape[0], HEAD_DIM), BF16)

    c2 = HEAD_DIM ** -0.5 * LOG2E
    group = HEADS // n_kv
    krows = slice(None)
    if has_bias:
        start, kind_id = 0, 0
        for blk in range(len(bias_plan.kind_of)):
            start = jnp.where(pl.program_id(1) == blk, bias_plan.key_start[blk], start)
            kind_id = jnp.where(pl.program_id(1) == blk, bias_plan.kind_of[blk], kind_id)
        krows = pl.ds(pl.multiple_of(start, 256), bias_plan.window)
    s1, s2, p1, p2 = {}, {}, {}, {}
    for h in range(HEADS):
        cols = slice(h // group * HEAD_DIM, (h // group + 1) * HEAD_DIM)
        q = q_ref[:, h * HEAD_DIM:(h + 1) * HEAD_DIM]
        if qk_norm:
            q = _head_rms(q.astype(F32), qn_ref[...])
        if rope:
            q = _rope(q.astype(F32), cq_ref[...], slq_ref[...], shq_ref[...])
        qb = q.astype(BF16)
        s1[h] = _dot_nt(qb, kb_s[krows, cols] if prep_k else k_ref[krows, cols])
        if has_cache:
            s2[h] = _dot_nt(qb, kcb_s[:, cols])
    for h in range(HEADS):
        if has_bias:
            t1 = s1[h] * c2 + bias_s[kind_id, h]
            mx = jnp.max(t1, axis=-1, keepdims=True)
            if has_cache:
                t2 = s2[h] * c2
                mx = jnp.maximum(mx, jnp.max(t2, axis=-1, keepdims=True))
                p2[h] = jnp.exp2(t2 - mx).astype(BF16)
            p1[h] = jnp.exp2(t1 - mx).astype(BF16)
        else:
            mx = jnp.max(s1[h], axis=-1, keepdims=True)
            if has_cache:
                mx = jnp.maximum(mx, jnp.max(s2[h], axis=-1, keepdims=True))
                p2[h] = jnp.exp2((s2[h] - mx) * c2).astype(BF16)
            p1[h] = jnp.exp2((s1[h] - mx) * c2).astype(BF16)
    for h in range(HEADS):
        hc = slice(h * HEAD_DIM, (h + 1) * HEAD_DIM)
        v1cols = slice(2 * (h // group) * HEAD_DIM, (2 * (h // group) + 2) * HEAD_DIM)
        acc = _dot(p1[h], vb_s[krows, v1cols])
        if has_cache:
            acc = acc + _dot(p2[h], vcb_s[:, v1cols])
        o = acc[:, 0:HEAD_DIM] / acc[:, HEAD_DIM:HEAD_DIM + 1]
        o_ref[:, hc] = (o * _silu(g_ref[:, hc].astype(F32))).astype(BF16)


def _attention(p_main, batch, seq, *, col_q, col_g, k_spec, v_spec, n_kv, cache=None, bias=None,
               qk_norm=None, rope=None, kv_out=None, bb=1):
    tq = 256
    nq = seq // tq
    kvw = n_kv * HEAD_DIM
    assert batch % bb == 0 and (bb == 1 or (nq == 1 and cache is None and bias is None and rope is None))
    in_specs = [pl.BlockSpec((bb * tq, GROUP_W), lambda b, i: (b * nq + i, col_q)),
                pl.BlockSpec((bb * seq, kvw), lambda b, i: (b, k_spec)),
                pl.BlockSpec((bb * seq, kvw), lambda b, i: (b, v_spec)),
                pl.BlockSpec((bb * tq, GROUP_W), lambda b, i: (b * nq + i, col_g))]
    args = [p_main, p_main, p_main, p_main]
    scratch = [pltpu.VMEM((bb * seq, kvw), BF16)] if (qk_norm is not None or rope is not None) else []
    scratch.append(pltpu.VMEM((bb * seq, 2 * kvw), BF16))
    if cache is not None:
        kc, vc, layer = cache
        lc = kc.shape[2] // n_kv
        in_specs += [pl.BlockSpec((None, None, lc * n_kv, HEAD_DIM), lambda b, i: (b, layer, 0, 0))] * 2
        args += [kc, vc]
        scratch += [pltpu.VMEM((lc, kvw), BF16), pltpu.VMEM((lc, 2 * kvw), BF16)]
    bias_plan = None
    if bias is not None:
        pairs, layer_b, bias_plan = bias
        in_specs.append(pl.BlockSpec((None,) + pairs.shape[1:], lambda b, i: (layer_b, 0, 0, 0, 0)))
        args.append(pairs)
        scratch.append(pltpu.VMEM((len(bias_plan.kinds), HEADS, tq, bias_plan.window), F32))
    if qk_norm is not None:
        in_specs += [pl.BlockSpec((1, HEAD_DIM), lambda b, i: (0, 0))] * 2
        args += list(qk_norm)
    if rope is not None:
        in_specs += [pl.BlockSpec((tq, HEAD_DIM), lambda b, i: (i, 0))] * 3
        in_specs += [pl.BlockSpec((seq, HEAD_DIM), lambda b, i: (0, 0))] * 3
        args += list(rope) + list(rope)
    out_shape = [jax.ShapeDtypeStruct((batch * seq, GROUP_W), BF16)]
    out_specs = [pl.BlockSpec((bb * tq, GROUP_W), lambda b, i: (b * nq + i, 0))]
    aliases = {}
    if kv_out is not None:
        assert nq == 1
        k_all, v_all, layer_o = kv_out
        for j, arr in enumerate((k_all, v_all)):
            if not isinstance(arr, jax.ShapeDtypeStruct):
                aliases[len(args)] = 1 + j
                in_specs.append(pl.BlockSpec(memory_space=pl.ANY))
                args.append(arr)
            out_shape.append(jax.ShapeDtypeStruct(arr.shape, arr.dtype))
            out_specs.append(pl.BlockSpec((None if bb == 1 else bb, None, seq * n_kv, HEAD_DIM),
                                          lambda b, i: (b, layer_o, 0, 0)))
    return pl.pallas_call(
        functools.partial(_attn_kernel, n_kv=n_kv, has_cache=cache is not None, bias_plan=bias_plan,
                          qk_norm=qk_norm is not None, rope=rope is not None, write_kv=kv_out is not None,
                          n_alias=len(aliases), bb=bb),
        out_shape=tuple(out_shape),
        grid=(batch // bb, nq),
        in_specs=in_specs,
        out_specs=tuple(out_specs),
        scratch_shapes=scratch,
        input_output_aliases=aliases,
        compiler_params=_params("arbitrary", "arbitrary"),
        name="attention",
    )(*args)


def _mlstm_kernel(*refs, seq, hp, has_state, write_state, n_alias):
    it = iter(refs)
    q_ref, k_ref, v_ref, og_ref, g_ref, gates_ref, gb_ref, onw_ref = (next(it) for _ in range(8))
    if has_state:
        c0_ref, n0_ref, m0_ref = next(it), next(it), next(it)
    for _ in range(n_alias):
        next(it)
    out_ref = next(it)
    if write_state:
        cout_ref, nout_ref, mout_ref = next(it), next(it), next(it)
    ht_s = next(it)

    L = ML_CHUNK
    nc = seq // L
    hd = HEAD_DIM
    scale = hd ** -0.5
    gt = (gates_ref[...] + gb_ref[...]).T[0:4 * HEADS, :]
    gate_id = lax.broadcasted_iota(jnp.int32, gt.shape, 0)
    gt = jnp.where((gate_id // HEADS) % 2 == 1, _log_sigmoid(gt), gt) * LOG2E

    def head_of(j):
        return j if hp == HEADS else pl.program_id(1) * hp + j

    def gate_row(r):
        if isinstance(r, int):
            return gt[r:r + 1, :]
        return jnp.sum(jnp.where(gate_id == r, gt, 0.0), axis=0, keepdims=True)

    si = lax.broadcasted_iota(jnp.int32, (L, L), 0)
    ti = lax.broadcasted_iota(jnp.int32, (L, L), 1)
    pick3 = jnp.where(lax.broadcasted_iota(jnp.int32, (16, hd), 0) < 3, 1.0, 0.0).astype(BF16)
    row16 = lax.broadcasted_iota(jnp.int32, (16, L), 0)

    def stack3(x):
        x1, x2, x3 = (p.astype(F32) for p in _split3(x))
        stacked = jnp.where(row16 == 0, x1, jnp.where(row16 == 1, x2, jnp.where(row16 == 2, x3, 0.0)))
        return stacked.astype(BF16)

    masks = [si <= ti, si >= ti]
    tris = [jnp.where(mk, 1.0, 0.0).astype(BF16) for mk in masks]
    edges = [L - 1, 0]
    units = [(j, d) for j in range(hp) for d in range(2)]
    ii_row, ff_row, state = {}, {}, {}
    for j, d in units:
        h = head_of(j)
        ii_row[j, d] = gate_row(d * 8 + h)
        ff_row[j, d] = gate_row(d * 8 + HEADS + h)
        if has_state:
            state[j, d] = (c0_ref[d, j].T, n0_ref[d, pl.ds(h, 1), :], m0_ref[d, pl.ds(h, 1), 0:1] * LOG2E)
        else:
            state[j, d] = (jnp.zeros((hd, hd), F32), jnp.zeros((1, hd), F32), jnp.zeros((1, 1), F32))

    ones_cols = jnp.ones((L, hd), BF16)

    def load_chunk(j, c):
        hc = slice(j * hd, (j + 1) * hd)
        rws = slice(c * L, (c + 1) * L)
        k = k_ref[rws, hc].astype(F32)
        return ((q_ref[rws, hc].astype(F32) * scale).astype(BF16), k, k.astype(BF16),
                jnp.concatenate([v_ref[rws, hc].astype(BF16), ones_cols], axis=1))

    loaded = {(j, c): load_chunk(j, c) for j in range(hp) for c in range(nc)}
    for step in range(nc):
        chunk = [step, nc - 1 - step]
        rows = {u: slice(chunk[u[1]] * L, (chunk[u[1]] + 1) * L) for u in units}
        b, g, g_col, g_src, mx = {}, {}, {}, {}, {}
        for u in units:
            b3 = _dot(stack3(ff_row[u][:, rows[u]]), tris[u[1]])
            b[u] = b3[0:1, :] + b3[1:2, :] + b3[2:3, :]
            g[u] = ii_row[u][:, rows[u]] - b[u]
        for u in units:
            g_col[u] = _dot_tn(stack3(g[u]), pick3)
        for u in units:
            g_src[u] = jnp.where(masks[u[1]], jnp.concatenate([g_col[u]] * (L // hd), axis=1), NEG_INF)
            mx[u] = jnp.maximum(state[u][2], jnp.max(g_src[u], axis=0, keepdims=True))
        for u in units:
            j, d = u
            ct, n, m = state[u]
            qb, k, kb, v1 = loaded[j, chunk[d]]
            p = _dot_nt(kb, qb) * jnp.exp2(g_src[u] - mx[u])
            w_inter = jnp.exp2(m - mx[u])
            n16 = jnp.broadcast_to(n, (16, hd)).astype(BF16)
            den = jnp.sum(p, axis=0, keepdims=True) + w_inter * _dot_nt(n16, qb)[0:1, :]
            num = _dot_tn(v1[:, 0:hd], p.astype(BF16)) + w_inter * _dot_nt(ct.astype(BF16), qb)
            ht_s[d, j, :, rows[u]] = num / jnp.maximum(jnp.abs(den), jnp.exp2(-(b[u] + mx[u])))
            mx_end = mx[u][:, edges[d]:edges[d] + 1]
            w_prev = jnp.exp2(m - mx_end)
            ku = k * jnp.exp2(g_col[u] - mx_end)
            kv = _dot_tn(v1, ku.astype(BF16))
            state[u] = (w_prev * ct + kv[0:hd, :], w_prev * n + kv[hd:hd + 1, :],
                        b[u][:, edges[d]:edges[d] + 1] + mx_end)

    if write_state:
        for j, d in units:
            ct, n, m = state[j, d]
            cout_ref[d, j] = ct.T
            nout_ref[d, pl.ds(head_of(j), 1), :] = n
            mout_ref[d, pl.ds(head_of(j), 1), :] = jnp.broadcast_to(m * LN2, (1, hd))

    for j in range(hp):
        hc = slice(j * hd, (j + 1) * hd)
        hs = (ht_s[0, j] + ht_s[1, j]).T
        y = hs * lax.rsqrt(jnp.mean(hs * hs, axis=-1, keepdims=True) + NORM_EPS) * onw_ref[:, hc]
        out_ref[:, hc] = (y * _sigmoid(og_ref[:, hc].astype(F32)) * _silu(g_ref[:, hc].astype(F32))).astype(BF16)


def _mlstm(p_main, p_if, batch, seq, gate_b, out_norm_w, hp, state=None, state_out=None):
    hd = HEAD_DIM
    nh = HEADS // hp
    w = hp * hd
    col = lambda base: (lambda b, h: (b, base * nh + h))
    in_specs = [pl.BlockSpec((seq, w), col(COL_ML_Q)), pl.BlockSpec((seq, w), col(COL_ML_K)),
                pl.BlockSpec((seq, w), col(COL_ML_V)), pl.BlockSpec((seq, w), col(COL_ML_O)),
                pl.BlockSpec((seq, w), col(COL_ML_G)),
                pl.BlockSpec((seq, 128), lambda b, h: (b, 0)),
                pl.BlockSpec((1, 128), lambda b, h: (0, 0)),
                pl.BlockSpec((1, w), lambda b, h: (0, h))]
    args = [p_main] * 5 + [p_if, jnp.pad(gate_b, (0, 128 - gate_b.shape[0])).reshape(1, 128),
                           out_norm_w.reshape(1, GROUP_W)]
    if state is not None:
        c0, n0, m0, layer = state
        m0b = jnp.broadcast_to(m0[..., None], m0.shape + (hd,))
        in_specs += [pl.BlockSpec((None, None, 2, hp, hd, hd), lambda b, h: (b, layer, 0, h, 0, 0)),
                     pl.BlockSpec((None, None, 2, HEADS, hd), lambda b, h: (b, layer, 0, 0, 0)),
                     pl.BlockSpec((None, None, 2, HEADS, hd), lambda b, h: (b, layer, 0, 0, 0))]
        args += [c0, n0, m0b]
    out_shape = [jax.ShapeDtypeStruct((batch * seq, GROUP_W), BF16)]
    out_specs = [pl.BlockSpec((seq, w), lambda b, h: (b, h))]
    aliases = {}
    if state_out is not None:
        c_all, layer_o = state_out
        if not isinstance(c_all, jax.ShapeDtypeStruct):
            aliases[len(args)] = 1
            in_specs.append(pl.BlockSpec(memory_space=pl.ANY))
            args.append(c_all)
        out_shape += [jax.ShapeDtypeStruct(c_all.shape, c_all.dtype),
                      jax.ShapeDtypeStruct((batch, 2, HEADS, hd), F32),
                      jax.ShapeDtypeStruct((batch, 2, HEADS, hd), F32)]
        out_specs += [pl.BlockSpec((None, None, 2, hp, hd, hd), lambda b, h: (b, layer_o, 0, h, 0, 0)),
                      pl.BlockSpec((None, 2, HEADS, hd), lambda b, h: (b, 0, 0, 0)),
                      pl.BlockSpec((None, 2, HEADS, hd), lambda b, h: (b, 0, 0, 0))]
    return pl.pallas_call(
        functools.partial(_mlstm_kernel, seq=seq, hp=hp, has_state=state is not None,
                          write_state=state_out is not None, n_alias=len(aliases)),
        out_shape=tuple(out_shape),
        grid=(batch, nh),
        in_specs=in_specs,
        out_specs=tuple(out_specs),
        scratch_shapes=[pltpu.VMEM((2, hp, hd, seq), F32)],
        input_output_aliases=aliases,
        compiler_params=_params("parallel", "arbitrary"),
        name="mlstm",
    )(*args)


class BiasPlan(NamedTuple):
    kinds: tuple
    kind_of: tuple
    key_start: tuple
    window: int


NA_Q_ROWS = 4
NA_K_ROWS = 12


def _na_bias(rpb, seq):
    rows = seq // GRID_W
    wr = min(NA_ROWS, rows)
    assert rows % NA_Q_ROWS == 0 and rows >= NA_K_ROWS
    c = np.arange(GRID_W)
    c_start = np.clip(c - NA_COLS // 2, 0, GRID_W - NA_COLS)
    col_ok = (c[None, :] >= c_start[:, None]) & (c[None, :] < c_start[:, None] + NA_COLS)
    col_off = np.clip(c[None, :] - c[:, None] + NA_COLS - 1, 0, 2 * NA_COLS - 2)
    col_sel = (col_off[..., None] == np.arange(2 * NA_COLS - 1)).astype(np.float32)
    blocks = jnp.einsum('lhij,cdj->lhicd', rpb.astype(F32), col_sel, precision=lax.Precision.HIGHEST)
    blocks = jnp.where(col_ok, blocks * LOG2E, NEG_INF)
    blocks = jnp.concatenate([blocks, jnp.full_like(blocks[:, :, :1], NEG_INF)], axis=2)
    masked = 2 * NA_ROWS - 1
    nblk = rows // NA_Q_ROWS
    key_row0 = [int(min(np.clip(i * NA_Q_ROWS - wr // 2, 0, rows - wr), rows - NA_K_ROWS)) for i in range(nblk)]
    idx = np.full((nblk, NA_Q_ROWS, NA_K_ROWS), masked, np.int32)
    for i in range(nblk):
        for a in range(NA_Q_ROWS):
            qr = i * NA_Q_ROWS + a
            r_start = int(np.clip(qr - wr // 2, 0, rows - wr))
            assert key_row0[i] <= r_start and r_start + wr <= key_row0[i] + NA_K_ROWS
            for kk in range(NA_K_ROWS):
                kr = key_row0[i] + kk
                if r_start <= kr < r_start + wr:
                    idx[i, a, kk] = kr - qr + NA_ROWS - 1
    kinds, kind_of = np.unique(idx.reshape(nblk, -1), axis=0, return_inverse=True)
    pair_of = kinds.reshape(-1, NA_Q_ROWS, NA_K_ROWS // 2, 2)
    pairs, entry = np.unique(pair_of.reshape(-1, 2), axis=0, return_inverse=True)
    table = jnp.concatenate([blocks[:, :, pairs[:, 0]], blocks[:, :, pairs[:, 1]]], axis=-1)
    entry = np.ravel(entry).reshape(pair_of.shape[:3])
    plan = BiasPlan(kinds=tuple(tuple(tuple(int(e) for e in row) for row in kind) for kind in entry),
                    kind_of=tuple(int(t) for t in np.ravel(kind_of)),
                    key_start=tuple(r * GRID_W for r in key_row0), window=NA_K_ROWS * GRID_W)
    return table, plan


def _rope_tables(seq):
    t = jnp.arange(seq)
    nf = HEAD_DIM // 4
    inv = ROPE_THETA ** (-jnp.arange(nf, dtype=F32) / nf)
    zeros = jnp.zeros((seq, nf), F32)

    def half(pos):
        ang = pos.astype(F32)[:, None] * inv
        cs, sn = jnp.cos(ang), jnp.sin(ang)
        return jnp.concatenate([cs, cs], -1), jnp.concatenate([-sn, zeros], -1), jnp.concatenate([zeros, sn], -1)

    parts = [half(t // GRID_W), half(t % GRID_W)]
    return tuple(jnp.concatenate([parts[0][i], parts[1][i]], -1) for i in range(3))


def kernel(x_prompt, x_sample, cache_na_k, cache_na_v, cache_gqa_k, cache_gqa_v, state_lru, state_mlstm_C,
           state_mlstm_n, state_mlstm_m, c, c_ctx, norm_w, ada_w, ada_b, w_in, lru_conv_w, lru_conv_b, lru_wr,
           lru_br, lru_wi, lru_bi, lru_lambda, na_rpb, gqa_qnorm, gqa_knorm, ml_gate_b, ml_out_norm, w_out,
           final_norm_w):
    bp, tp, d = x_prompt.shape
    bs, ts, _ = x_sample.shape
    depth = w_in.shape[0]
    past = cache_na_k.shape[2]

    cond = jnp.zeros((COND_ROWS, d), F32).at[0].set(c_ctx).at[1:1 + bs].set(c)
    mod4 = _modulation(cond, ada_w, ada_b).reshape(depth, COND_ROWS, 1, 3 * d)

    w_in_t = jnp.swapaxes(w_in, 1, 2)
    w_out_b = w_out.astype(BF16)
    w_gates = jnp.concatenate([lru_wr[:, 0], lru_wi[:, 0], lru_wr[:, 1], lru_wi[:, 1]], axis=-1).astype(BF16)
    bias_pairs, bias_plan = _na_bias(na_rpb, ts)
    rope = _rope_tables(ts)
    zeros_h0 = jnp.zeros((bp, 2, GROUP_W), F32)

    xp = x_prompt.reshape(bp * tp, d)
    xs = x_sample.reshape(bs * ts, d)
    norm_w2 = norm_w.reshape(depth, 1, d)
    fnw = final_norm_w.reshape(1, d)
    na_k = na_v = jax.ShapeDtypeStruct((bp, depth, tp * HEADS, HEAD_DIM), F32)
    gqa_k = gqa_v = jax.ShapeDtypeStruct((bp, depth, tp * GQA_KV_HEADS, HEAD_DIM), F32)
    st_c = jax.ShapeDtypeStruct((bp, depth, 2, HEADS, HEAD_DIM, HEAD_DIM), F32)
    flat = lambda a: a.reshape(a.shape[0], a.shape[1], a.shape[2] * a.shape[3], a.shape[4])
    cache_na = (flat(cache_na_k), flat(cache_na_v))
    cache_gqa = (flat(cache_gqa_k), flat(cache_gqa_v))
    new = {k: [] for k in ('lru', 'n', 'm')}

    for l in range(depth):
        last = l == depth - 1
        lru_args = (lru_conv_w[l], lru_conv_b[l].reshape(1, GROUP_W), w_gates[l], lru_br[l], lru_bi[l], lru_lambda[l])
        qk_norm = (gqa_qnorm[l].reshape(1, HEAD_DIM), gqa_knorm[l].reshape(1, HEAD_DIM))

        p, pif = _in_projection(xp, norm_w2[l], mod4, l, 0, bp * tp, w_in_t)
        out_a, st_lru = _lru(p, bp, tp, *lru_args, zeros_h0, bb=CTX_LRU_BATCH)
        out_b, na_k, na_v = _attention(p, bp, tp, col_q=COL_NA_Q, col_g=COL_NA_G, k_spec=COL_NA_K,
                                       v_spec=COL_NA_V, n_kv=HEADS, kv_out=(na_k, na_v, l), bb=CTX_ATTN_BATCH)
        out_c, gqa_k, gqa_v = _attention(p, bp, tp, col_q=COL_GQA_Q, col_g=COL_GQA_G, k_spec=2 * COL_GQA_KV,
                                         v_spec=2 * COL_GQA_KV + 1, n_kv=GQA_KV_HEADS, qk_norm=qk_norm,
                                         kv_out=(gqa_k, gqa_v, l), bb=CTX_ATTN_BATCH)
        out_d, st_c, st_n, st_m = _mlstm(p, pif, bp, tp, ml_gate_b[l], ml_out_norm[l], HEADS, state_out=(st_c, l))
        xp = _out_projection((out_a, out_b, out_c, out_d), w_out_b[l], xp, mod4, l, 0, bp * tp,
                             fnw if last else None)
        new['lru'].append(st_lru)
        new['n'].append(st_n)
        new['m'].append(st_m[..., 0])

        p, pif = _in_projection(xs, norm_w2[l], mod4, l, 1, ts, w_in_t)
        out_a, _ = _lru(p, bs, ts, *lru_args, state_lru[:, l])
        (out_b,) = _attention(p, bs, ts, col_q=COL_NA_Q, col_g=COL_NA_G, k_spec=COL_NA_K, v_spec=COL_NA_V,
                              n_kv=HEADS, cache=cache_na + (l,), bias=(bias_pairs, l, bias_plan))
        (out_c,) = _attention(p, bs, ts, col_q=COL_GQA_Q, col_g=COL_GQA_G, k_spec=2 * COL_GQA_KV,
                              v_spec=2 * COL_GQA_KV + 1, n_kv=GQA_KV_HEADS, cache=cache_gqa + (l,),
                              qk_norm=qk_norm, rope=rope)
        (out_d,) = _mlstm(p, pif, bs, ts, ml_gate_b[l], ml_out_norm[l], HEADS,
                          state=(state_mlstm_C, state_mlstm_n, state_mlstm_m, l))
        xs = _out_projection((out_a, out_b, out_c, out_d), w_out_b[l], xs, mod4, l, 1, ts,
                             fnw if last else None)

    stacked = {k: jnp.stack(v, axis=1) for k, v in new.items()}
    heads = lambda a, n: a.reshape(bp, depth, tp, n, HEAD_DIM)
    return (xp.reshape(bp, tp, d), xs.reshape(bs, ts, d), heads(na_k, HEADS), heads(na_v, HEADS),
            heads(gqa_k, GQA_KV_HEADS), heads(gqa_v, GQA_KV_HEADS),
            stacked['lru'], st_c, stacked['n'], stacked['m'])
```
